```python
import math
import jax, jax.numpy as jnp
from jax import lax
import numpy as np

D_MODEL = 1024
BATCH = 8
SEQ = 2048
DEPTH = 2
DEC_BATCH = 128
DEC_SEQ = 1
PAST_LEN = 16384
PAGE_SIZE = 128

D_CONV = D_MODEL
CONV_W = 3
N_HEADS = 4
D_MLSTM = 2 * D_MODEL
DV = D_MLSTM // N_HEADS
DK = DV // 2
D_QK = N_HEADS * DK
CHUNK = 64
D_FF = 2816
EPS = 1e-6
IN_SIZES = (D_CONV, D_CONV, D_CONV, D_QK, D_QK, D_MLSTM, D_MLSTM, N_HEADS, N_HEADS, D_MODEL, D_MODEL)
IN_COLS = 3 * D_CONV + 2 * D_QK + 2 * D_MLSTM + 2 * N_HEADS + 2 * D_MODEL

kernel_name = 'gated_conv_mlstm_macaron_step'


def rmsnorm(x, g):
    xf = x.astype(jnp.float32)
    y = xf * lax.rsqrt(jnp.mean(xf * xf, axis=-1, keepdims=True) + EPS)
    return (y * g.astype(jnp.float32)).astype(x.dtype)


def swiglu(x, w_gu, w_down):
    g, u = jnp.split(x @ w_gu, 2, axis=-1)
    return (jax.nn.silu(g) * u) @ w_down


def short_conv(u, buf, w):
    S = u.shape[1]
    full = jnp.concatenate([buf.astype(u.dtype), u], axis=1)
    y = full[:, 0:S] * w[0]
    for j in range(1, CONV_W):
        y = y + full[:, j:j + S] * w[j]
    return y, full[:, S:]


def _mlstm_chunk(carry, inp):
    C, n, m = carry
    qc, kc, vc, ic, fc = inp
    L = qc.shape[2]
    b = jnp.cumsum(fc, axis=-1)
    causal = jnp.tril(jnp.ones((L, L), dtype=bool))
    dlog = jnp.where(causal, b[..., :, None] - b[..., None, :] + ic[..., None, :], -jnp.inf)
    inter = b + m[..., None]
    m_row = jnp.maximum(inter, jnp.max(dlog, axis=-1))
    s = jnp.einsum('bhtk,bhsk->bhts', qc, kc) * jnp.exp(dlog - m_row[..., None])
    sc = jnp.exp(inter - m_row)
    num = sc[..., None] * jnp.einsum('bhtk,bhkv->bhtv', qc, C) + jnp.einsum('bhts,bhsv->bhtv', s, vc)
    den = sc * jnp.einsum('bhtk,bhk->bht', qc, n) + jnp.sum(s, axis=-1)
    h = num / jnp.maximum(jnp.abs(den), jnp.exp(-m_row))[..., None]
    b_last = b[..., -1]
    wlog = b_last[..., None] - b + ic
    m_new = jnp.maximum(b_last + m, jnp.max(wlog, axis=-1))
    decay = jnp.exp(b_last + m - m_new)
    wk = jnp.exp(wlog - m_new[..., None])[..., None] * kc
    C_new = decay[..., None, None] * C + jnp.einsum('bhsk,bhsv->bhkv', wk, vc)
    n_new = decay[..., None] * n + jnp.sum(wk, axis=2)
    return (C_new, n_new, m_new), h


def mlstm(q, k, v, i_pre, log_f, C0, n0, m0):
    Bn, S = q.shape[0], q.shape[1]
    L = math.gcd(S, CHUNK)
    NC = S // L
    f32 = jnp.float32
    def chunks(a):
        a = a.astype(f32).reshape((Bn, NC, L) + a.shape[2:])
        return jnp.moveaxis(jnp.moveaxis(a, 1, 0), 2, 3) if a.ndim == 5 else jnp.transpose(a, (1, 0, 3, 2))
    inp = (chunks(q * (DK ** -0.5)), chunks(k), chunks(v), chunks(i_pre), chunks(log_f))
    carry0 = (C0.astype(f32), n0.astype(f32), m0.astype(f32))
    (C, n, m), hs = lax.scan(_mlstm_chunk, carry0, inp)
    h = jnp.transpose(hs, (1, 0, 3, 2, 4)).reshape(Bn, S, N_HEADS, DV)
    return h, C.astype(C0.dtype), n.astype(n0.dtype), m.astype(m0.dtype)


def token_mixer(h, conv_buf, C0, n0, m0, w_in, w_conv, b_igate, b_fgate, g_head, w_proj_a, w_proj_b, w_out):
    Bn, S, _ = h.shape
    z = h @ w_in
    idx, acc = [], 0
    for s_ in IN_SIZES[:-1]:
        acc += s_
        idx.append(acc)
    gb, gc, xa, q, k, v, o, ip, fp, ga, gm = jnp.split(z, idx, axis=-1)
    yc, new_buf = short_conv(gc * xa, conv_buf, w_conv)
    ya = (gb * yc) @ w_proj_a
    q = q.reshape(Bn, S, N_HEADS, DK)
    k = k.reshape(Bn, S, N_HEADS, DK)
    v = v.reshape(Bn, S, N_HEADS, DV)
    i_pre = (ip + b_igate).astype(jnp.float32)
    log_f = jax.nn.log_sigmoid((fp + b_fgate).astype(jnp.float32))
    hb, C, n, m = mlstm(q, k, v, i_pre, log_f, C0, n0, m0)
    hb = hb * lax.rsqrt(jnp.mean(hb * hb, axis=-1, keepdims=True) + EPS)
    hb = (hb * g_head.reshape(N_HEADS, DV).astype(jnp.float32)).reshape(Bn, S, D_MLSTM).astype(h.dtype)
    yb = (jax.nn.sigmoid(o) * hb) @ w_proj_b
    mix = jax.nn.sigmoid(ga) * ya + jax.nn.sigmoid(gm) * yb
    return mix @ w_out, new_buf, C, n, m


def layer(x, conv_buf, C0, n0, m0, g_ffn1, w_ffn1_gu, w_ffn1_down, g_mix, w_in, w_conv, b_igate, b_fgate,
          g_head, w_proj_a, w_proj_b, w_out, g_ffn2, w_ffn2_gu, w_ffn2_down):
    x = x + 0.5 * swiglu(rmsnorm(x, g_ffn1), w_ffn1_gu, w_ffn1_down)
    y, nb, C, n, m = token_mixer(rmsnorm(x, g_mix), conv_buf, C0, n0, m0, w_in, w_conv, b_igate, b_fgate,
                                 g_head, w_proj_a, w_proj_b, w_out)
    x = x + y
    x = x + 0.5 * swiglu(rmsnorm(x, g_ffn2), w_ffn2_gu, w_ffn2_down)
    return x, nb, C, n, m


def setup_inputs(seed: int = 0) -> dict:
    key = jax.random.key(seed)
    ks = jax.random.split(key, 24)
    nrm = lambda k, shp, s: jax.random.normal(k, shp, jnp.float32) * s
    gain = lambda k, shp: 1.0 + 0.05 * jax.random.normal(k, shp, jnp.float32)
    return {
        'x_prompt': nrm(ks[0], (BATCH, SEQ, D_MODEL), 1.0),
        'x_sample': nrm(ks[1], (DEC_BATCH, DEC_SEQ, D_MODEL), 1.0),
        'state_conv': nrm(ks[2], (DEPTH, DEC_BATCH, CONV_W - 1, D_CONV), 1.0),
        'state_mlstm_c': nrm(ks[3], (DEPTH, DEC_BATCH, N_HEADS, DK, DV), 1.0),
        'state_mlstm_n': nrm(ks[4], (DEPTH, DEC_BATCH, N_HEADS, DK), 1.0),
        'state_mlstm_m': nrm(ks[5], (DEPTH, DEC_BATCH, N_HEADS), 1.0),
        'g_ffn1': gain(ks[6], (DEPTH, D_MODEL)),
        'w_ffn1_gu': nrm(ks[7], (DEPTH, D_MODEL, 2 * D_FF), D_MODEL ** -0.5),
        'w_ffn1_down': nrm(ks[8], (DEPTH, D_FF, D_MODEL), D_FF ** -0.5),
        'g_mix': gain(ks[9], (DEPTH, D_MODEL)),
        'w_in': nrm(ks[10], (DEPTH, D_MODEL, IN_COLS), D_MODEL ** -0.5),
        'w_conv': nrm(ks[11], (DEPTH, CONV_W, D_CONV), CONV_W ** -0.5),
        'b_igate': nrm(ks[12], (DEPTH, N_HEADS), 0.1),
        'b_fgate': jnp.linspace(3.0, 6.0, N_HEADS, dtype=jnp.float32) + nrm(ks[13], (DEPTH, N_HEADS), 0.1),
        'g_head': gain(ks[14], (DEPTH, D_MLSTM)),
        'w_proj_a': nrm(ks[15], (DEPTH, D_CONV, D_MODEL), D_CONV ** -0.5),
        'w_proj_b': nrm(ks[16], (DEPTH, D_MLSTM, D_MODEL), D_MLSTM ** -0.5),
        'w_out': nrm(ks[17], (DEPTH, D_MODEL, D_MODEL), D_MODEL ** -0.5),
        'g_ffn2': gain(ks[18], (DEPTH, D_MODEL)),
        'w_ffn2_gu': nrm(ks[19], (DEPTH, D_MODEL, 2 * D_FF), D_MODEL ** -0.5),
        'w_ffn2_down': nrm(ks[20], (DEPTH, D_FF, D_MODEL), D_FF ** -0.5),
        'g_final': gain(ks[21], (D_MODEL,)),
    }


def reference(x_prompt, x_sample, state_conv, state_mlstm_c, state_mlstm_n, state_mlstm_m,
              g_ffn1, w_ffn1_gu, w_ffn1_down, g_mix, w_in, w_conv, b_igate, b_fgate, g_head,
              w_proj_a, w_proj_b, w_out, g_ffn2, w_ffn2_gu, w_ffn2_down, g_final):
    bp = x_prompt.shape[0]
    dt = x_prompt.dtype
    xp, xs = x_prompt, x_sample
    conv_p, c_p, n_p, m_p = [], [], [], []
    conv_s, c_s, n_s, m_s = [], [], [], []
    for l in range(DEPTH):
        w = (g_ffn1[l], w_ffn1_gu[l], w_ffn1_down[l], g_mix[l], w_in[l], w_conv[l], b_igate[l], b_fgate[l],
             g_head[l], w_proj_a[l], w_proj_b[l], w_out[l], g_ffn2[l], w_ffn2_gu[l], w_ffn2_down[l])
        xp, nb, C, n, m = layer(xp, jnp.zeros((bp, CONV_W - 1, D_CONV), dt),
                                jnp.zeros((bp, N_HEADS, DK, DV), dt), jnp.zeros((bp, N_HEADS, DK), dt),
                                jnp.zeros((bp, N_HEADS), dt), *w)
        conv_p.append(nb); c_p.append(C); n_p.append(n); m_p.append(m)
        xs, nb, C, n, m = layer(xs, state_conv[l], state_mlstm_c[l], state_mlstm_n[l], state_mlstm_m[l], *w)
        conv_s.append(nb); c_s.append(C); n_s.append(n); m_s.append(m)
    y_prompt = rmsnorm(xp, g_final)
    y_sample = rmsnorm(xs, g_final)
    return (y_prompt, y_sample,
            jnp.stack(conv_p), jnp.stack(c_p), jnp.stack(n_p), jnp.stack(m_p),
            jnp.stack(conv_s), jnp.stack(c_s), jnp.stack(n_s), jnp.stack(m_s))
```

```python
import functools

import jax
import jax.numpy as jnp
from jax import lax
from jax.experimental import pallas as pl
from jax.experimental.pallas import tpu as pltpu

F32 = jnp.float32
BF16 = jnp.bfloat16

N_HEADS = 4
CONV_W = 3
EPS = 1e-6
MLSTM_CHUNK = 256
GATE_LANES = 128
TRANSPOSE_ROWS = 128
VMEM_LIMIT_BYTES = 56 * 1024 * 1024


def _params(n_axes):
    return pltpu.CompilerParams(dimension_semantics=("arbitrary",) * n_axes,
                                vmem_limit_bytes=VMEM_LIMIT_BYTES)


def _rmsnorm(x, g):
    return x * lax.rsqrt(jnp.mean(x * x, axis=-1, keepdims=True) + EPS) * g


def _dot(a, b):
    return jnp.dot(a, b, preferred_element_type=F32)


def _ffn_kernel(x_ref, g_ref, wg_ref, wu_ref, wd_ref, gf_ref, o_ref, h_scr, acc_scr, *, n_f, final_norm):
    f = pl.program_id(1)

    @pl.when(f == 0)
    def _():
        h_scr[...] = _rmsnorm(x_ref[...], g_ref[...]).astype(BF16)

    h = h_scr[...]
    gate = _dot(h, wg_ref[...])
    up = _dot(h, wu_ref[...])
    act = (gate * jax.nn.sigmoid(gate) * up).astype(BF16)
    part = _dot(act, wd_ref[...])

    @pl.when(f == 0)
    def _():
        acc_scr[...] = part

    @pl.when(f > 0)
    def _():
        acc_scr[...] += part

    @pl.when(f == n_f - 1)
    def _():
        y = x_ref[...] + 0.5 * acc_scr[...]
        if final_norm:
            y = _rmsnorm(y, gf_ref[...])
        o_ref[...] = y


def _ffn(x, g, w_gu, w_down, g_final, *, tm, tf, final_norm):
    t, d = x.shape
    d_ff = w_down.shape[0]
    n_f = d_ff // tf
    kern = functools.partial(_ffn_kernel, n_f=n_f, final_norm=final_norm)
    return pl.pallas_call(
        kern,
        grid=(t // tm, n_f),
        in_specs=[
            pl.BlockSpec((tm, d), lambda i, f: (i, 0)),
            pl.BlockSpec((1, d), lambda i, f: (0, 0)),
            pl.BlockSpec((d, tf), lambda i, f: (0, f)),
            pl.BlockSpec((d, tf), lambda i, f: (0, n_f + f)),
            pl.BlockSpec((tf, d), lambda i, f: (f, 0)),
            pl.BlockSpec((1, d), lambda i, f: (0, 0)),
        ],
        out_specs=pl.BlockSpec((tm, d), lambda i, f: (i, 0)),
        out_shape=jax.ShapeDtypeStruct((t, d), F32),
        scratch_shapes=[pltpu.VMEM((tm, d), BF16), pltpu.VMEM((tm, d), F32)],
        compiler_params=_params(2),
        name="ffn",
    )(x, g, w_gu, w_gu, w_down, g_final)


def _conv_seq_kernel(x_ref, g_ref, wb_ref, wc_ref, wa_ref, wconv_ref, ya_ref, tail_ref, ubuf, *, tm, tiles_per_seq):
    i = pl.program_id(0)
    h = _rmsnorm(x_ref[...], g_ref[...]).astype(BF16)
    zb = _dot(h, wb_ref[...])
    u = _dot(h, wc_ref[...]) * _dot(h, wa_ref[...])

    @pl.when(i % tiles_per_seq == 0)
    def _():
        ubuf[0:8, :] = jnp.zeros((8, ubuf.shape[1]), F32)

    @pl.when(i % tiles_per_seq != 0)
    def _():
        ubuf[0:8, :] = ubuf[tm:tm + 8, :]

    ubuf[8:tm + 8, :] = u
    w = wconv_ref[...]
    yc = ubuf[6:tm + 6, :] * w[0:1, :] + ubuf[7:tm + 7, :] * w[1:2, :] + u * w[2:3, :]
    ya_ref[...] = (zb * yc).astype(BF16)
    tail_ref[0] = u[tm - 8:tm, :]


def _conv_proj_seq(x, g, w_conv3, w_conv, *, tm, seq_len):
    t, d = x.shape
    dc = w_conv.shape[1]
    tiles_per_seq = seq_len // tm
    kern = functools.partial(_conv_seq_kernel, tm=tm, tiles_per_seq=tiles_per_seq)
    wspec = lambda c: pl.BlockSpec((d, dc), lambda i: (0, c))
    return pl.pallas_call(
        kern,
        grid=(t // tm,),
        in_specs=[
            pl.BlockSpec((tm, d), lambda i: (i, 0)),
            pl.BlockSpec((1, d), lambda i: (0, 0)),
            wspec(0), wspec(1), wspec(2),
            pl.BlockSpec((CONV_W, dc), lambda i: (0, 0)),
        ],
        out_specs=[
            pl.BlockSpec((tm, dc), lambda i: (i, 0)),
            pl.BlockSpec((1, 8, dc), lambda i: (i // tiles_per_seq, 0, 0)),
        ],
        out_shape=[jax.ShapeDtypeStruct((t, dc), BF16),
                   jax.ShapeDtypeStruct((t // seq_len, 8, dc), F32)],
        scratch_shapes=[pltpu.VMEM((tm + 8, dc), F32)],
        compiler_params=_params(1),
        name="conv_proj_seq",
    )(x, g, w_conv3, w_conv3, w_conv3, w_conv)


def _conv_step_kernel(x_ref, g_ref, wb_ref, wc_ref, wa_ref, wconv_ref, b0_ref, b1_ref, ya_ref, u_ref):
    h = _rmsnorm(x_ref[...], g_ref[...]).astype(BF16)
    zb = _dot(h, wb_ref[...])
    u = _dot(h, wc_ref[...]) * _dot(h, wa_ref[...])
    w = wconv_ref[...]
    yc = b0_ref[...] * w[0:1, :] + b1_ref[...] * w[1:2, :] + u * w[2:3, :]
    ya_ref[...] = (zb * yc).astype(BF16)
    u_ref[...] = u


def _conv_proj_step(x, g, w_conv3, w_conv, buf0, buf1):
    t, d = x.shape
    dc = w_conv.shape[1]
    wspec = lambda c: pl.BlockSpec((d, dc), lambda i: (0, c))
    row = pl.BlockSpec((t, dc), lambda i: (0, 0))
    return pl.pallas_call(
        _conv_step_kernel,
        grid=(1,),
        in_specs=[
            pl.BlockSpec((t, d), lambda i: (0, 0)),
            pl.BlockSpec((1, d), lambda i: (0, 0)),
            wspec(0), wspec(1), wspec(2),
            pl.BlockSpec((CONV_W, dc), lambda i: (0, 0)),
            row, row,
        ],
        out_specs=[row, row],
        out_shape=[jax.ShapeDtypeStruct((t, dc), BF16), jax.ShapeDtypeStruct((t, dc), F32)],
        compiler_params=_params(1),
        name="conv_proj_step",
    )(x, g, w_conv3, w_conv3, w_conv3, w_conv, buf0, buf1)


def _log_sigmoid(x):
    return jnp.minimum(x, 0.0) - jnp.log(1.0 + jnp.exp(-jnp.abs(x)))


def _mix_proj_kernel(x_ref, g_ref, w_ref, wif_ref, bias_ref, z_ref, gate_ref, h_scr, *, tn, n_plain, q_cols, q_scale):
    j = pl.program_id(1)

    @pl.when(j == 0)
    def _():
        h = _rmsnorm(x_ref[...], g_ref[...]).astype(BF16)
        h_scr[...] = h
        gz = _dot(h, wif_ref[...]) + bias_ref[...]
        lane = lax.broadcasted_iota(jnp.int32, gz.shape, 1)
        is_forget = (lane >= N_HEADS) & (lane < 2 * N_HEADS)
        gate_ref[...] = jnp.where(is_forget, _log_sigmoid(gz), gz)

    z = _dot(h_scr[...], w_ref[...])

    @pl.when(j < n_plain)
    def _():
        col = j * tn + lax.broadcasted_iota(jnp.int32, (1, tn), 1)
        z_ref[...] = (z * jnp.where(col < q_cols, q_scale, 1.0)).astype(z_ref.dtype)

    @pl.when(j >= n_plain)
    def _():
        z_ref[...] = jax.nn.sigmoid(z).astype(z_ref.dtype)


def _mix_proj(x, g, w2, w_if, bias_if, *, tm, tn, plain_cols, q_cols, q_scale, out_dtype):
    t, d = x.shape
    n = w2.shape[1]
    kern = functools.partial(_mix_proj_kernel, tn=tn, n_plain=plain_cols // tn, q_cols=q_cols, q_scale=q_scale)
    return pl.pallas_call(
        kern,
        grid=(t // tm, n // tn),
        in_specs=[
            pl.BlockSpec((tm, d), lambda i, j: (i, 0)),
            pl.BlockSpec((1, d), lambda i, j: (0, 0)),
            pl.BlockSpec((d, tn), lambda i, j: (0, j)),
            pl.BlockSpec((d, GATE_LANES), lambda i, j: (0, 0)),
            pl.BlockSpec((1, GATE_LANES), lambda i, j: (0, 0)),
        ],
        out_specs=[
            pl.BlockSpec((tm, tn), lambda i, j: (i, j)),
            pl.BlockSpec((tm, GATE_LANES), lambda i, j: (i, 0)),
        ],
        out_shape=[jax.ShapeDtypeStruct((t, n), out_dtype),
                   jax.ShapeDtypeStruct((t, GATE_LANES), F32)],
        scratch_shapes=[pltpu.VMEM((tm, d), BF16)],
        compiler_params=_params(2),
        name="mix_proj",
    )(x, g, w2, w_if, bias_if)


def _mlstm_chunk_kernel(q_ref, k_ref, v_ref, so_ref, gt_ref, gh_ref,
                        yb_ref, c_out, n_out, m_out, c_scr, n_scr, m_scr, *, lc, n_chunks):
    c = pl.program_id(2)

    @pl.when(c == 0)
    def _():
        c_scr[...] = jnp.zeros(c_scr.shape, F32)
        n_scr[...] = jnp.zeros(n_scr.shape, F32)
        m_scr[...] = jnp.zeros(m_scr.shape, F32)

    q = q_ref[...]
    k = k_ref[...]
    v = v_ref[...]
    i_row = gt_ref[0, 0:1, :]
    lf_row = gt_ref[0, 1:2, :]
    m_prev = m_scr[...]

    t_idx = lax.broadcasted_iota(jnp.int32, (lc, lc), 0)
    s_idx = lax.broadcasted_iota(jnp.int32, (lc, lc), 1)
    causal = s_idx <= t_idx
    eye = s_idx == t_idx

    b_col = jnp.sum(jnp.where(causal, lf_row, 0.0), axis=1, keepdims=True)
    b_row = jnp.sum(jnp.where(eye, b_col, 0.0), axis=0, keepdims=True)
    b_last = jnp.sum(lf_row, axis=1, keepdims=True)

    dlog = jnp.where(causal, b_col - b_row + i_row, -jnp.inf)
    inter = b_col + m_prev
    m_row = jnp.maximum(inter, jnp.max(dlog, axis=1, keepdims=True))
    dmat = jnp.exp(dlog - m_row)
    s = lax.dot_general(q, k, (((1,), (1,)), ((), ())), preferred_element_type=F32) * dmat
    sc = jnp.exp(inter - m_row)

    c_old = c_scr[...]
    n_old = n_scr[...]
    num = sc * _dot(q, c_old.astype(BF16)) + _dot(s.astype(BF16), v)
    qn = jnp.sum(q.astype(F32) * n_old, axis=1, keepdims=True)
    den = sc * qn + jnp.sum(s, axis=1, keepdims=True)
    hraw = num / jnp.maximum(jnp.abs(den), jnp.exp(-m_row))

    hn = hraw * lax.rsqrt(jnp.mean(hraw * hraw, axis=1, keepdims=True) + EPS) * gh_ref[...]
    yb_ref[...] = (so_ref[...].astype(F32) * hn).astype(BF16)

    wlog_row = b_last - b_row + i_row
    m_new = jnp.maximum(b_last + m_prev, jnp.max(wlog_row, axis=1, keepdims=True))
    decay = jnp.exp(b_last + m_prev - m_new)
    w_row = jnp.exp(wlog_row - m_new)
    w_col = jnp.sum(jnp.where(eye, w_row, 0.0), axis=1, keepdims=True)
    wk = w_col * k.astype(F32)
    c_new = decay * c_old + lax.dot_general(wk.astype(BF16), v, (((0,), (0,)), ((), ())),
                                            preferred_element_type=F32)
    n_new = decay * n_old + jnp.sum(wk, axis=0, keepdims=True)
    c_scr[...] = c_new
    n_scr[...] = n_new
    m_scr[...] = m_new

    @pl.when(c == n_chunks - 1)
    def _():
        c_out[0, 0] = c_new
        n_out[0, 0] = jnp.broadcast_to(n_new, n_out.shape[2:])
        m_out[0, 0] = jnp.broadcast_to(m_new, m_out.shape[2:])


def _mlstm_chunked(z2, gates_t, g_head, *, n_seq, seq_len, dk, dv):
    lc = MLSTM_CHUNK
    n_chunks = seq_len // lc
    t = n_seq * seq_len
    k_blk0 = N_HEADS
    v_blk0 = (2 * N_HEADS * dk) // dv
    so_blk0 = v_blk0 + N_HEADS
    row = lambda b, h, c: b * n_chunks + c
    kern = functools.partial(_mlstm_chunk_kernel, lc=lc, n_chunks=n_chunks)
    return pl.pallas_call(
        kern,
        grid=(n_seq, N_HEADS, n_chunks),
        in_specs=[
            pl.BlockSpec((lc, dk), lambda b, h, c: (row(b, h, c), h)),
            pl.BlockSpec((lc, dk), lambda b, h, c: (row(b, h, c), k_blk0 + h)),
            pl.BlockSpec((lc, dv), lambda b, h, c: (row(b, h, c), v_blk0 + h)),
            pl.BlockSpec((lc, dv), lambda b, h, c: (row(b, h, c), so_blk0 + h)),
            pl.BlockSpec((1, 2, lc), lambda b, h, c: (h, 0, row(b, h, c))),
            pl.BlockSpec((1, dv), lambda b, h, c: (0, h)),
        ],
        out_specs=[
            pl.BlockSpec((lc, dv), lambda b, h, c: (row(b, h, c), h)),
            pl.BlockSpec((1, 1, dk, dv), lambda b, h, c: (b, h, 0, 0)),
            pl.BlockSpec((1, 1, 8, dk), lambda b, h, c: (b, h, 0, 0)),
            pl.BlockSpec((1, 1, 8, 128), lambda b, h, c: (b, h, 0, 0)),
        ],
        out_shape=[
            jax.ShapeDtypeStruct((t, N_HEADS * dv), BF16),
            jax.ShapeDtypeStruct((n_seq, N_HEADS, dk, dv), F32),
            jax.ShapeDtypeStruct((n_seq, N_HEADS, 8, dk), F32),
            jax.ShapeDtypeStruct((n_seq, N_HEADS, 8, 128), F32),
        ],
        scratch_shapes=[pltpu.VMEM((dk, dv), F32), pltpu.VMEM((1, dk), F32), pltpu.VMEM((1, 1), F32)],
        compiler_params=_params(3),
        name="mlstm_chunked",
    )(z2, z2, z2, z2, gates_t, g_head)


def _mlstm_step_kernel(q_ref, k_ref, v_ref, so_ref, gate_ref, gh_ref, c_ref, n_ref, m_ref, *rest, bb):
    yb_ref, c_out, n_out, m_out = rest[-4:]
    hd = pl.program_id(1)
    q = q_ref[...]
    k = k_ref[...]
    v = v_ref[...]
    n_old = n_ref[...]

    gates = gate_ref[...]
    lane = lax.broadcasted_iota(jnp.int32, gates.shape, 1)
    i_pre = jnp.sum(jnp.where(lane == hd, gates, 0.0), axis=1, keepdims=True)
    log_f = jnp.sum(jnp.where(lane == hd + N_HEADS, gates, 0.0), axis=1, keepdims=True)
    m_all = m_ref[...]
    mlane = lax.broadcasted_iota(jnp.int32, m_all.shape, 1)
    m_old = jnp.sum(jnp.where(mlane == hd, m_all, 0.0), axis=1, keepdims=True)

    inter = log_f + m_old
    m_new = jnp.maximum(inter, i_pre)
    s_w = jnp.exp(i_pre - m_new)
    sc = jnp.exp(inter - m_new)
    s = jnp.sum(q * k, axis=1, keepdims=True) * s_w
    den = sc * jnp.sum(q * n_old, axis=1, keepdims=True) + s
    denom = jnp.maximum(jnp.abs(den), jnp.exp(-m_new))

    wk = s_w * k
    pad = jnp.zeros((TRANSPOSE_ROWS - bb, q.shape[1]), F32)
    q_t = jnp.transpose(jnp.concatenate([q, pad], axis=0))
    wk_t = jnp.transpose(jnp.concatenate([wk, pad], axis=0))
    qc_rows = []
    for j in range(bb):
        c_old = c_ref[0, j, 0]
        qc_rows.append(jnp.sum(q_t[:, j:j + 1] * c_old, axis=0, keepdims=True))
        c_out[0, j, 0] = sc[j:j + 1, :] * c_old + wk_t[:, j:j + 1] * v[j:j + 1, :]
    qc = jnp.concatenate(qc_rows, axis=0)

    hraw = (sc * qc + s * v) / denom
    hn = hraw * lax.rsqrt(jnp.mean(hraw * hraw, axis=1, keepdims=True) + EPS) * gh_ref[...]
    yb_ref[...] = (so_ref[...] * hn).astype(BF16)
    n_out[...] = sc * n_old + wk
    m_out[0] = jnp.broadcast_to(m_new, m_out.shape[1:])


def _mlstm_step(z2, gates, g_head, c_state, n0, m0, c_acc, *, layer, dk, dv, bb=8):
    nb = z2.shape[0]
    k_blk0 = N_HEADS
    v_blk0 = (2 * N_HEADS * dk) // dv
    so_blk0 = v_blk0 + N_HEADS
    kern = functools.partial(_mlstm_step_kernel, bb=bb)
    c_spec = pl.BlockSpec((1, bb, 1, dk, dv), lambda b, h: (layer, b, h, 0, 0))
    in_specs = [
        pl.BlockSpec((bb, dk), lambda b, h: (b, h)),
        pl.BlockSpec((bb, dk), lambda b, h: (b, k_blk0 + h)),
        pl.BlockSpec((bb, dv), lambda b, h: (b, v_blk0 + h)),
        pl.BlockSpec((bb, dv), lambda b, h: (b, so_blk0 + h)),
        pl.BlockSpec((bb, GATE_LANES), lambda b, h: (b, 0)),
        pl.BlockSpec((1, dv), lambda b, h: (0, h)),
        c_spec,
        pl.BlockSpec((bb, dk), lambda b, h: (b, h)),
        pl.BlockSpec((bb, N_HEADS), lambda b, h: (b, 0)),
    ]
    args = [z2, z2, z2, z2, gates, g_head, c_state, n0, m0]
    aliases = {}
    if c_acc is not None:
        in_specs.append(pl.BlockSpec(memory_space=pl.ANY))
        args.append(c_acc)
        aliases = {len(args) - 1: 1}
    return pl.pallas_call(
        kern,
        grid=(nb // bb, N_HEADS),
        in_specs=in_specs,
        out_specs=[
            pl.BlockSpec((bb, dv), lambda b, h: (b, h)),
            c_spec,
            pl.BlockSpec((bb, dk), lambda b, h: (b, h)),
            pl.BlockSpec((1, bb, 128), lambda b, h: (h, b, 0)),
        ],
        out_shape=[
            jax.ShapeDtypeStruct((nb, N_HEADS * dv), BF16),
            jax.ShapeDtypeStruct(c_state.shape, F32),
            jax.ShapeDtypeStruct((nb, N_HEADS * dk), F32),
            jax.ShapeDtypeStruct((N_HEADS, nb, 128), F32),
        ],
        input_output_aliases=aliases,
        compiler_params=_params(2),
        name="mlstm_step",
    )(*args)


def _out_proj_kernel(ya_ref, yb_ref, sa_ref, sb_ref, x_ref, wa_ref, wb_ref, wo_ref, o_ref):
    ya = _dot(ya_ref[...], wa_ref[...])
    yb = _dot(yb_ref[...], wb_ref[...])
    mix = sa_ref[...].astype(F32) * ya + sb_ref[...].astype(F32) * yb
    o_ref[...] = x_ref[...] + _dot(mix.astype(BF16), wo_ref[...])


def _out_proj(ya_in, yb_in, z2, x, w_a, w_b, w_o, *, tm, gate_col0):
    t, d = x.shape
    da, db = ya_in.shape[1], yb_in.shape[1]
    ga_blk = gate_col0 // d
    const = lambda shape: pl.BlockSpec(shape, lambda i: (0, 0))
    return pl.pallas_call(
        _out_proj_kernel,
        grid=(t // tm,),
        in_specs=[
            pl.BlockSpec((tm, da), lambda i: (i, 0)),
            pl.BlockSpec((tm, db), lambda i: (i, 0)),
            pl.BlockSpec((tm, d), lambda i: (i, ga_blk)),
            pl.BlockSpec((tm, d), lambda i: (i, ga_blk + 1)),
            pl.BlockSpec((tm, d), lambda i: (i, 0)),
            const((da, d)), const((db, d)), const((d, d)),
        ],
        out_specs=pl.BlockSpec((tm, d), lambda i: (i, 0)),
        out_shape=jax.ShapeDtypeStruct((t, d), F32),
        compiler_params=_params(1),
        name="out_proj",
    )(ya_in, yb_in, z2, z2, x, w_a, w_b, w_o)


def kernel(x_prompt, x_sample, state_conv, state_mlstm_c, state_mlstm_n, state_mlstm_m, g_ffn1, w_ffn1_gu, w_ffn1_down, g_mix, w_in, w_conv, b_igate, b_fgate, g_head, w_proj_a, w_proj_b, w_out, g_ffn2, w_ffn2_gu, w_ffn2_down, g_final):
    n_seq, seq_len, d = x_prompt.shape
    n_dec = x_sample.shape[0]
    depth = w_in.shape[0]
    d_conv = w_conv.shape[2]
    d_mlstm = g_head.shape[1]
    dv = d_mlstm // N_HEADS
    dk = state_mlstm_n.shape[3]
    d_qk = N_HEADS * dk
    d_ff = w_ffn1_down.shape[1]
    assert x_sample.shape[1] == 1 and state_conv.shape[2] == CONV_W - 1

    c_q = 3 * d_conv
    c_if = c_q + 2 * d_qk + 2 * d_mlstm
    c_ga = c_if + 2 * N_HEADS
    plain_cols = 2 * d_qk + d_mlstm
    q_scale = float(dk) ** -0.5

    tm_p, tm_s = 512, n_dec
    tf = d_ff // 2
    tn = 2048

    xp = x_prompt.reshape(n_seq * seq_len, d)
    xs = x_sample.reshape(n_dec, d)
    row = lambda a: a.reshape(1, -1)
    g_fin = row(g_final)

    conv_p, c_p, n_p, m_p = [], [], [], []
    conv_s, c_s, n_s, m_s = [], None, [], []
    for l in range(depth):
        last = l == depth - 1
        w1_gu, w1_d = w_ffn1_gu[l].astype(BF16), w_ffn1_down[l].astype(BF16)
        w2_gu, w2_d = w_ffn2_gu[l].astype(BF16), w_ffn2_down[l].astype(BF16)
        w_conv3 = w_in[l, :, :c_q].astype(BF16)
        w_mix = jnp.concatenate([w_in[l, :, c_q:c_if], w_in[l, :, c_ga:]], axis=1).astype(BF16)
        w_if = jnp.pad(w_in[l, :, c_if:c_ga], ((0, 0), (0, GATE_LANES - 2 * N_HEADS))).astype(BF16)
        bias_if = jnp.pad(jnp.concatenate([b_igate[l], b_fgate[l]]), (0, GATE_LANES - 2 * N_HEADS)).reshape(1, -1)
        w_a, w_b, w_o = w_proj_a[l].astype(BF16), w_proj_b[l].astype(BF16), w_out[l].astype(BF16)
        g1, gm, g2, gh = row(g_ffn1[l]), row(g_mix[l]), row(g_ffn2[l]), row(g_head[l])
        mix_kw = dict(tn=tn, plain_cols=plain_cols, q_cols=d_qk, q_scale=q_scale)

        xp = _ffn(xp, g1, w1_gu, w1_d, g_fin, tm=tm_p, tf=tf, final_norm=False)
        ya_in, tail = _conv_proj_seq(xp, gm, w_conv3, w_conv[l], tm=tm_p, seq_len=seq_len)
        z2, gates = _mix_proj(xp, gm, w_mix, w_if, bias_if, tm=tm_p, out_dtype=BF16, **mix_kw)
        gates_t = jnp.transpose(gates[:, :2 * N_HEADS]).reshape(2, N_HEADS, -1).transpose(1, 0, 2)
        yb_in, c_new, n_new, m_new = _mlstm_chunked(z2, gates_t, gh, n_seq=n_seq, seq_len=seq_len, dk=dk, dv=dv)
        xp = _out_proj(ya_in, yb_in, z2, xp, w_a, w_b, w_o, tm=tm_p, gate_col0=plain_cols + d_mlstm)
        xp = _ffn(xp, g2, w2_gu, w2_d, g_fin, tm=tm_p, tf=tf, final_norm=last)
        conv_p.append(tail[:, 8 - (CONV_W - 1):, :])
        c_p.append(c_new)
        n_p.append(n_new[:, :, 0, :])
        m_p.append(m_new[:, :, 0, 0])

        xs = _ffn(xs, g1, w1_gu, w1_d, g_fin, tm=tm_s, tf=tf, final_norm=False)
        ya_in, u_new = _conv_proj_step(xs, gm, w_conv3, w_conv[l], state_conv[l, :, 0, :], state_conv[l, :, 1, :])
        z2, gates = _mix_proj(xs, gm, w_mix, w_if, bias_if, tm=tm_s, out_dtype=F32, **mix_kw)
        yb_in, c_s, n_new, m_new = _mlstm_step(z2, gates, gh, state_mlstm_c, state_mlstm_n[l].reshape(n_dec, d_qk),
                                               state_mlstm_m[l], c_s, layer=l, dk=dk, dv=dv)
        xs = _out_proj(ya_in, yb_in, z2, xs, w_a, w_b, w_o, tm=tm_s, gate_col0=plain_cols + d_mlstm)
        xs = _ffn(xs, g2, w2_gu, w2_d, g_fin, tm=tm_s, tf=tf, final_norm=last)
        conv_s.append(jnp.stack([state_conv[l, :, 1, :], u_new], axis=1))
        n_s.append(n_new.reshape(n_dec, N_HEADS, dk))
        m_s.append(jnp.transpose(m_new[:, :, 0]))

    return (xp.reshape(n_seq, seq_len, d), xs.reshape(n_dec, 1, d),
            jnp.stack(conv_p), jnp.stack(c_p), jnp.stack(n_p), jnp.stack(m_p),
            jnp.stack(conv_s), c_s, jnp.stack(n_s), jnp.stack(m_s))
```

```python
import functools

import jax
import jax.numpy as jnp
from jax import lax
from jax.experimental import pallas as pl
from jax.experimental.pallas import tpu as pltpu

F32 = jnp.float32
BF16 = jnp.bfloat16

N_HEADS = 4
CONV_W = 3
EPS = 1e-6
MLSTM_CHUNK = 256
GATE_LANES = 128
TRANSPOSE_ROWS = 128
MXU_COLS = 256
COL_CHUNK = 2 * MXU_COLS
ROW_TILE = 512
VMEM_LIMIT_BYTES = 56 * 1024 * 1024


def _params(n_axes):
    return pltpu.CompilerParams(dimension_semantics=("arbitrary",) * n_axes,
                                vmem_limit_bytes=VMEM_LIMIT_BYTES)


def _resident(shape, layer):
    zeros = (0,) * (len(shape) - 1)
    return pl.BlockSpec(shape, lambda *_: (layer,) + zeros, pipeline_mode=pl.Buffered(1))


def _chunks(n, step):
    return [(c, min(c + step, n)) for c in range(0, n, step)]


def _rmsnorm(x, g):
    return x * lax.rsqrt(jnp.mean(x * x, axis=-1, keepdims=True) + EPS) * g


def _dot(a, b):
    return jnp.dot(a, b, preferred_element_type=F32)


def _sigmoid(x):
    return 0.5 * jnp.tanh(0.5 * x) + 0.5


def _ffn_kernel(x_ref, g_ref, wgu_ref, wd_ref, gf_ref, o_ref, act_scr, *, final_norm):
    d = x_ref.shape[1]
    d_ff = wd_ref.shape[0]
    x = x_ref[...]
    h = _rmsnorm(x, g_ref[...]).astype(BF16)
    for c0, c1 in _chunks(d_ff, COL_CHUNK):
        gate = _dot(h, wgu_ref[:, c0:c1])
        up = _dot(h, wgu_ref[:, d_ff + c0:d_ff + c1])
        act_scr[:, c0:c1] = (gate * _sigmoid(gate) * up).astype(BF16)
    act = act_scr[...]
    for c0, c1 in _chunks(d, COL_CHUNK):
        o_ref[:, c0:c1] = x[:, c0:c1] + 0.5 * _dot(act, wd_ref[:, c0:c1])
    if final_norm:
        o_ref[...] = _rmsnorm(o_ref[...], gf_ref[...])


def _ffn(x, g, w_gu, w_down, g_final, *, layer, tm, final_norm):
    t, d = x.shape
    d_ff = w_down.shape[1]
    kern = functools.partial(_ffn_kernel, final_norm=final_norm)
    return pl.pallas_call(
        kern,
        grid=(t // tm,),
        in_specs=[
            pl.BlockSpec((tm, d), lambda i: (i, 0)),
            _resident((None, 1, d), layer),
            _resident((None, d, 2 * d_ff), layer),
            _resident((None, d_ff, d), layer),
            pl.BlockSpec((1, d), lambda i: (0, 0)),
        ],
        out_specs=pl.BlockSpec((tm, d), lambda i: (i, 0)),
        out_shape=jax.ShapeDtypeStruct((t, d), F32),
        scratch_shapes=[pltpu.VMEM((tm, d_ff), BF16)],
        compiler_params=_params(1),
        name="ffn",
    )(x, g, w_gu, w_down, g_final)


def _conv_seq_kernel(x_ref, g_ref, w3_ref, wconv_ref, ya_ref, tail_ref, ubuf, *, tm, tiles_per_seq):
    i = pl.program_id(0)
    dc = ya_ref.shape[1]

    @pl.when(i % tiles_per_seq == 0)
    def _():
        ubuf[0:8, :] = jnp.zeros((8, dc), F32)

    @pl.when(i % tiles_per_seq != 0)
    def _():
        ubuf[0:8, :] = ubuf[tm:tm + 8, :]

    h = _rmsnorm(x_ref[...], g_ref[...]).astype(BF16)
    w = wconv_ref[...]
    for c0, c1 in _chunks(dc, COL_CHUNK):
        zb = _dot(h, w3_ref[:, c0:c1])
        u = _dot(h, w3_ref[:, dc + c0:dc + c1]) * _dot(h, w3_ref[:, 2 * dc + c0:2 * dc + c1])
        ubuf[8:tm + 8, c0:c1] = u
        yc = (ubuf[6:tm + 6, c0:c1] * w[0:1, c0:c1] + ubuf[7:tm + 7, c0:c1] * w[1:2, c0:c1]
              + u * w[2:3, c0:c1])
        ya_ref[:, c0:c1] = (zb * yc).astype(BF16)
        tail_ref[0, :, c0:c1] = u[tm - 8:tm, :]


def _conv_proj_seq(x, g, w_conv3, w_conv, *, layer, tm, seq_len):
    t, d = x.shape
    dc = w_conv.shape[2]
    tiles_per_seq = seq_len // tm
    kern = functools.partial(_conv_seq_kernel, tm=tm, tiles_per_seq=tiles_per_seq)
    return pl.pallas_call(
        kern,
        grid=(t // tm,),
        in_specs=[
            pl.BlockSpec((tm, d), lambda i: (i, 0)),
            _resident((None, 1, d), layer),
            _resident((None, d, 3 * dc), layer),
            _resident((None, CONV_W, dc), layer),
        ],
        out_specs=[
            pl.BlockSpec((tm, dc), lambda i: (i, 0)),
            pl.BlockSpec((1, 8, dc), lambda i: (i // tiles_per_seq, 0, 0)),
        ],
        out_shape=[jax.ShapeDtypeStruct((t, dc), BF16),
                   jax.ShapeDtypeStruct((t // seq_len, 8, dc), F32)],
        scratch_shapes=[pltpu.VMEM((tm + 8, dc), F32)],
        compiler_params=_params(1),
        name="conv_proj_seq",
    )(x, g, w_conv3, w_conv)


def _conv_step_kernel(x_ref, g_ref, w3_ref, wconv_ref, b0_ref, b1_ref, ya_ref, u_ref):
    dc = ya_ref.shape[1]
    h = _rmsnorm(x_ref[...], g_ref[...]).astype(BF16)
    zb = _dot(h, w3_ref[:, 0:dc])
    u = _dot(h, w3_ref[:, dc:2 * dc]) * _dot(h, w3_ref[:, 2 * dc:3 * dc])
    w = wconv_ref[...]
    yc = b0_ref[...] * w[0:1, :] + b1_ref[...] * w[1:2, :] + u * w[2:3, :]
    ya_ref[...] = (zb * yc).astype(BF16)
    u_ref[...] = u


def _conv_proj_step(x, g, w_conv3, w_conv, buf0, buf1, *, layer):
    t, d = x.shape
    dc = w_conv.shape[2]
    row = pl.BlockSpec((t, dc), lambda i: (0, 0))
    return pl.pallas_call(
        _conv_step_kernel,
        grid=(1,),
        in_specs=[
            pl.BlockSpec((t, d), lambda i: (0, 0)),
            _resident((None, 1, d), layer),
            _resident((None, d, 3 * dc), layer),
            _resident((None, CONV_W, dc), layer),
            row, row,
        ],
        out_specs=[row, row],
        out_shape=[jax.ShapeDtypeStruct((t, dc), BF16), jax.ShapeDtypeStruct((t, dc), F32)],
        compiler_params=_params(1),
        name="conv_proj_step",
    )(x, g, w_conv3, w_conv, buf0, buf1)


def _log_sigmoid(x):
    return jnp.minimum(x, 0.0) - jnp.log(1.0 + jnp.exp(-jnp.abs(x)))


def _mix_proj_kernel(x_ref, g_ref, wm_ref, wt_ref, bias_ref, z_ref, gate_ref, *, plain_cols, q_cols, q_scale):
    n_main = wm_ref.shape[1]
    n_sig_tail = wt_ref.shape[1] - GATE_LANES
    h = _rmsnorm(x_ref[...], g_ref[...]).astype(BF16)

    gz = _dot(h, wt_ref[:, n_sig_tail:]) + bias_ref[...]
    lane = lax.broadcasted_iota(jnp.int32, gz.shape, 1)
    is_forget = (lane >= N_HEADS) & (lane < 2 * N_HEADS)
    gate_ref[...] = jnp.where(is_forget, _log_sigmoid(gz), gz)

    for c0, c1 in _chunks(n_main + n_sig_tail, COL_CHUNK):
        z = _dot(h, wm_ref[:, c0:c1]) if c0 < n_main else _dot(h, wt_ref[:, c0 - n_main:c1 - n_main])
        if c0 < q_cols:
            z = z * q_scale
        elif c0 >= plain_cols:
            z = _sigmoid(z)
        z_ref[:, c0:c1] = z.astype(z_ref.dtype)


def _mix_proj(x, g, w_main, w_tail, bias_if, *, layer, tm, plain_cols, q_cols, q_scale, out_dtype):
    t, d = x.shape
    n_main, n_tail = w_main.shape[2], w_tail.shape[2]
    n = n_main + n_tail - GATE_LANES
    assert q_cols % COL_CHUNK == 0 and plain_cols % COL_CHUNK == 0 and n_main % COL_CHUNK == 0
    kern = functools.partial(_mix_proj_kernel, plain_cols=plain_cols, q_cols=q_cols, q_scale=q_scale)
    return pl.pallas_call(
        kern,
        grid=(t // tm,),
        in_specs=[
            pl.BlockSpec((tm, d), lambda i: (i, 0)),
            _resident((None, 1, d), layer),
            _resident((None, d, n_main), layer),
            _resident((None, d, n_tail), layer),
            _resident((None, 1, GATE_LANES), layer),
        ],
        out_specs=[
            pl.BlockSpec((tm, n), lambda i: (i, 0)),
            pl.BlockSpec((tm, GATE_LANES), lambda i: (i, 0)),
        ],
        out_shape=[jax.ShapeDtypeStruct((t, n), out_dtype),
                   jax.ShapeDtypeStruct((t, GATE_LANES), F32)],
        compiler_params=_params(1),
        name="mix_proj",
    )(x, g, w_main, w_tail, bias_if)


def _mlstm_chunk_kernel(q_ref, k_ref, v_ref, so_ref, gt_ref, gh_ref, *rest, lc, n_chunks, dk, dv):
    yb_ref, c_out, n_out, m_out, c_scr, n_scr, m_scr = rest[-7:]
    c = pl.program_id(1)

    @pl.when(c == 0)
    def _():
        c_scr[...] = jnp.zeros(c_scr.shape, F32)
        n_scr[...] = jnp.zeros(n_scr.shape, F32)
        m_scr[...] = jnp.zeros(m_scr.shape, F32)

    t_idx = lax.broadcasted_iota(jnp.int32, (lc, lc), 0)
    s_idx = lax.broadcasted_iota(jnp.int32, (lc, lc), 1)
    causal = s_idx <= t_idx
    eye = s_idx == t_idx

    for hd in range(N_HEADS):
        q = q_ref[:, hd * dk:(hd + 1) * dk]
        k = k_ref[:, hd * dk:(hd + 1) * dk]
        v = v_ref[:, hd * dv:(hd + 1) * dv]
        i_row = gt_ref[hd, 0:1, :]
        lf_row = gt_ref[hd, 1:2, :]
        m_prev = m_scr[hd]

        b_col = jnp.sum(jnp.where(causal, lf_row, 0.0), axis=1, keepdims=True)
        b_row = jnp.sum(jnp.where(eye, b_col, 0.0), axis=0, keepdims=True)
        b_last = jnp.sum(lf_row, axis=1, keepdims=True)

        dlog = jnp.where(causal, b_col - b_row + i_row, -jnp.inf)
        inter = b_col + m_prev
        m_row = jnp.maximum(inter, jnp.max(dlog, axis=1, keepdims=True))
        dmat = jnp.exp(dlog - m_row)
        s = lax.dot_general(q, k, (((1,), (1,)), ((), ())), preferred_element_type=F32) * dmat
        sc = jnp.exp(inter - m_row)

        c_old = c_scr[hd]
        n_old = n_scr[hd]
        num = sc * _dot(q, c_old.astype(BF16)) + _dot(s.astype(BF16), v)
        qn = jnp.sum(q.astype(F32) * n_old, axis=1, keepdims=True)
        den = sc * qn + jnp.sum(s, axis=1, keepdims=True)
        hraw = num / jnp.maximum(jnp.abs(den), jnp.exp(-m_row))

        hn = (hraw * lax.rsqrt(jnp.mean(hraw * hraw, axis=1, keepdims=True) + EPS)
              * gh_ref[:, hd * dv:(hd + 1) * dv])
        yb_ref[:, hd * dv:(hd + 1) * dv] = (so_ref[:, hd * dv:(hd + 1) * dv].astype(F32) * hn).astype(BF16)

        wlog_row = b_last - b_row + i_row
        m_new = jnp.maximum(b_last + m_prev, jnp.max(wlog_row, axis=1, keepdims=True))
        decay = jnp.exp(b_last + m_prev - m_new)
        w_row = jnp.exp(wlog_row - m_new)
        w_col = jnp.sum(jnp.where(eye, w_row, 0.0), axis=1, keepdims=True)
        wk = w_col * k.astype(F32)
        c_new = decay * c_old + lax.dot_general(wk.astype(BF16), v, (((0,), (0,)), ((), ())),
                                                preferred_element_type=F32)
        n_new = decay * n_old + jnp.sum(wk, axis=0, keepdims=True)
        c_scr[hd] = c_new
        n_scr[hd] = n_new
        m_scr[hd] = m_new

    @pl.when(c == n_chunks - 1)
    def _():
        c_out[...] = c_scr[...]
        for hd in range(N_HEADS):
            n_out[hd] = jnp.broadcast_to(n_scr[hd], n_out.shape[1:])
            m_out[hd] = jnp.broadcast_to(m_scr[hd], m_out.shape[1:])


def _mlstm_chunked(z2, gates_t, g_head, c_acc, *, layer, depth, n_seq, seq_len, dk, dv):
    lc = MLSTM_CHUNK
    n_chunks = seq_len // lc
    t = n_seq * seq_len
    d_qk, d_v = N_HEADS * dk, N_HEADS * dv
    assert d_v == 2 * d_qk
    row = lambda b, c: b * n_chunks + c
    kern = functools.partial(_mlstm_chunk_kernel, lc=lc, n_chunks=n_chunks, dk=dk, dv=dv)
    in_specs = [
        pl.BlockSpec((lc, d_qk), lambda b, c: (row(b, c), 0)),
        pl.BlockSpec((lc, d_qk), lambda b, c: (row(b, c), 1)),
        pl.BlockSpec((lc, d_v), lambda b, c: (row(b, c), 1)),
        pl.BlockSpec((lc, d_v), lambda b, c: (row(b, c), 2)),
        pl.BlockSpec((N_HEADS, 2, lc), lambda b, c: (0, 0, row(b, c))),
        _resident((None, 1, d_v), layer),
    ]
    args = [z2, z2, z2, z2, gates_t, g_head]
    aliases = {}
    if c_acc is not None:
        in_specs.append(pl.BlockSpec(memory_space=pl.ANY))
        args.append(c_acc)
        aliases = {len(args) - 1: 1}
    return pl.pallas_call(
        kern,
        grid=(n_seq, n_chunks),
        in_specs=in_specs,
        out_specs=[
            pl.BlockSpec((lc, d_v), lambda b, c: (row(b, c), 0)),
            pl.BlockSpec((None, None, N_HEADS, dk, dv), lambda b, c: (layer, b, 0, 0, 0)),
            pl.BlockSpec((None, N_HEADS, 8, dk), lambda b, c: (b, 0, 0, 0)),
            pl.BlockSpec((None, N_HEADS, 8, 128), lambda b, c: (b, 0, 0, 0)),
        ],
        out_shape=[
            jax.ShapeDtypeStruct((t, d_v), BF16),
            jax.ShapeDtypeStruct((depth, n_seq, N_HEADS, dk, dv), F32),
            jax.ShapeDtypeStruct((n_seq, N_HEADS, 8, dk), F32),
            jax.ShapeDtypeStruct((n_seq, N_HEADS, 8, 128), F32),
        ],
        scratch_shapes=[pltpu.VMEM((N_HEADS, dk, dv), F32), pltpu.VMEM((N_HEADS, 1, dk), F32),
                        pltpu.VMEM((N_HEADS, 1, 1), F32)],
        input_output_aliases=aliases,
        compiler_params=_params(2),
        name="mlstm_chunked",
    )(*args)


def _mlstm_step_kernel(q_ref, k_ref, v_ref, so_ref, gate_ref, gh_ref, c_ref, n_ref, m_ref, *rest, bb):
    yb_ref, c_out, n_out, m_out = rest[-4:]
    hd = pl.program_id(1)
    q = q_ref[...]
    k = k_ref[...]
    v = v_ref[...]
    n_old = n_ref[...]

    gates = gate_ref[...]
    lane = lax.broadcasted_iota(jnp.int32, gates.shape, 1)
    i_pre = jnp.sum(jnp.where(lane == hd, gates, 0.0), axis=1, keepdims=True)
    log_f = jnp.sum(jnp.where(lane == hd + N_HEADS, gates, 0.0), axis=1, keepdims=True)
    m_all = m_ref[...]
    mlane = lax.broadcasted_iota(jnp.int32, m_all.shape, 1)
    m_old = jnp.sum(jnp.where(mlane == hd, m_all, 0.0), axis=1, keepdims=True)

    inter = log_f + m_old
    m_new = jnp.maximum(inter, i_pre)
    s_w = jnp.exp(i_pre - m_new)
    sc = jnp.exp(inter - m_new)
    s = jnp.sum(q * k, axis=1, keepdims=True) * s_w
    den = sc * jnp.sum(q * n_old, axis=1, keepdims=True) + s
    denom = jnp.maximum(jnp.abs(den), jnp.exp(-m_new))

    wk = s_w * k
    pad = jnp.zeros((TRANSPOSE_ROWS - bb, q.shape[1]), F32)
    q_t = jnp.transpose(jnp.concatenate([q, pad], axis=0))
    wk_t = jnp.transpose(jnp.concatenate([wk, pad], axis=0))
    qc_rows = []
    for j in range(bb):
        c_old = c_ref[j, 0]
        qc_rows.append(jnp.sum(q_t[:, j:j + 1] * c_old, axis=0, keepdims=True))
        c_out[j, 0] = sc[j:j + 1, :] * c_old + wk_t[:, j:j + 1] * v[j:j + 1, :]
    qc = jnp.concatenate(qc_rows, axis=0)

    hraw = (sc * qc + s * v) / denom
    hn = hraw * lax.rsqrt(jnp.mean(hraw * hraw, axis=1, keepdims=True) + EPS) * gh_ref[...]
    yb_ref[...] = (so_ref[...] * hn).astype(BF16)
    n_out[...] = sc * n_old + wk
    m_out[0] = jnp.broadcast_to(m_new, m_out.shape[1:])


def _mlstm_step(z2, gates, g_head, c_state, n0, m0, c_acc, *, layer, dk, dv, bb=8):
    nb = z2.shape[0]
    k_blk0 = N_HEADS
    v_blk0 = (2 * N_HEADS * dk) // dv
    so_blk0 = v_blk0 + N_HEADS
    kern = functools.partial(_mlstm_step_kernel, bb=bb)
    c_spec = pl.BlockSpec((None, bb, 1, dk, dv), lambda b, h: (layer, b, h, 0, 0))
    in_specs = [
        pl.BlockSpec((bb, dk), lambda b, h: (b, h)),
        pl.BlockSpec((bb, dk), lambda b, h: (b, k_blk0 + h)),
        pl.BlockSpec((bb, dv), lambda b, h: (b, v_blk0 + h)),
        pl.BlockSpec((bb, dv), lambda b, h: (b, so_blk0 + h)),
        pl.BlockSpec((bb, GATE_LANES), lambda b, h: (b, 0)),
        pl.BlockSpec((None, 1, dv), lambda b, h: (layer, 0, h)),
        c_spec,
        pl.BlockSpec((None, bb, dk), lambda b, h: (layer, b, h)),
        pl.BlockSpec((None, bb, N_HEADS), lambda b, h: (layer, b, 0)),
    ]
    args = [z2, z2, z2, z2, gates, g_head, c_state, n0, m0]
    aliases = {}
    if c_acc is not None:
        in_specs.append(pl.BlockSpec(memory_space=pl.ANY))
        args.append(c_acc)
        aliases = {len(args) - 1: 1}
    return pl.pallas_call(
        kern,
        grid=(nb // bb, N_HEADS),
        in_specs=in_specs,
        out_specs=[
            pl.BlockSpec((bb, dv), lambda b, h: (b, h)),
            c_spec,
            pl.BlockSpec((bb, dk), lambda b, h: (b, h)),
            pl.BlockSpec((1, bb, 128), lambda b, h: (h, b, 0)),
        ],
        out_shape=[
            jax.ShapeDtypeStruct((nb, N_HEADS * dv), BF16),
            jax.ShapeDtypeStruct(c_state.shape, F32),
            jax.ShapeDtypeStruct((nb, N_HEADS * dk), F32),
            jax.ShapeDtypeStruct((N_HEADS, nb, 128), F32),
        ],
        input_output_aliases=aliases,
        compiler_params=_params(2),
        name="mlstm_step",
    )(*args)


def _out_proj_kernel(ya_ref, yb_ref, sa_ref, sb_ref, x_ref, wa_ref, wb_ref, wo_ref, o_ref):
    d = o_ref.shape[1]
    ya = _dot(ya_ref[...], wa_ref[...])
    yb = _dot(yb_ref[...], wb_ref[...])
    mix = (sa_ref[...].astype(F32) * ya + sb_ref[...].astype(F32) * yb).astype(BF16)
    for c0, c1 in _chunks(d, COL_CHUNK):
        o_ref[:, c0:c1] = x_ref[:, c0:c1] + _dot(mix, wo_ref[:, c0:c1])


def _out_proj(ya_in, yb_in, z2, x, w_a, w_b, w_o, *, layer, tm, gate_col0):
    t, d = x.shape
    da, db = ya_in.shape[1], yb_in.shape[1]
    ga_blk = gate_col0 // d
    return pl.pallas_call(
        _out_proj_kernel,
        grid=(t // tm,),
        in_specs=[
            pl.BlockSpec((tm, da), lambda i: (i, 0)),
            pl.BlockSpec((tm, db), lambda i: (i, 0)),
            pl.BlockSpec((tm, d), lambda i: (i, ga_blk)),
            pl.BlockSpec((tm, d), lambda i: (i, ga_blk + 1)),
            pl.BlockSpec((tm, d), lambda i: (i, 0)),
            _resident((None, da, d), layer), _resident((None, db, d), layer), _resident((None, d, d), layer),
        ],
        out_specs=pl.BlockSpec((tm, d), lambda i: (i, 0)),
        out_shape=jax.ShapeDtypeStruct((t, d), F32),
        compiler_params=_params(1),
        name="out_proj",
    )(ya_in, yb_in, z2, z2, x, w_a, w_b, w_o)


def kernel(x_prompt, x_sample, state_conv, state_mlstm_c, state_mlstm_n, state_mlstm_m, g_ffn1, w_ffn1_gu, w_ffn1_down, g_mix, w_in, w_conv, b_igate, b_fgate, g_head, w_proj_a, w_proj_b, w_out, g_ffn2, w_ffn2_gu, w_ffn2_down, g_final):
    n_seq, seq_len, d = x_prompt.shape
    n_dec = x_sample.shape[0]
    depth = w_in.shape[0]
    d_conv = w_conv.shape[2]
    d_mlstm = g_head.shape[1]
    dv = d_mlstm // N_HEADS
    dk = state_mlstm_n.shape[3]
    d_qk = N_HEADS * dk
    assert x_sample.shape[1] == 1 and state_conv.shape[2] == CONV_W - 1

    c_q = 3 * d_conv
    c_if = c_q + 2 * d_qk + 2 * d_mlstm
    c_ga = c_if + 2 * N_HEADS
    plain_cols = 2 * d_qk + d_mlstm
    gate_col0 = plain_cols + d_mlstm
    q_scale = float(dk) ** -0.5
    gate_pad = GATE_LANES - 2 * N_HEADS

    w1_gu, w1_d = w_ffn1_gu.astype(BF16), w_ffn1_down.astype(BF16)
    w2_gu, w2_d = w_ffn2_gu.astype(BF16), w_ffn2_down.astype(BF16)
    w_conv3 = w_in[:, :, :c_q].astype(BF16)
    w_main = w_in[:, :, c_q:c_if].astype(BF16)
    w_tail = jnp.pad(jnp.concatenate([w_in[:, :, c_ga:], w_in[:, :, c_if:c_ga]], axis=2),
                     ((0, 0), (0, 0), (0, gate_pad))).astype(BF16)
    bias_if = jnp.pad(jnp.concatenate([b_igate, b_fgate], axis=1), ((0, 0), (0, gate_pad)))[:, None, :]
    w_a, w_b, w_o = w_proj_a.astype(BF16), w_proj_b.astype(BF16), w_out.astype(BF16)
    g1, gmx, g2, gh = g_ffn1[:, None, :], g_mix[:, None, :], g_ffn2[:, None, :], g_head[:, None, :]
    g_fin = g_final.reshape(1, -1)
    n_state = state_mlstm_n.reshape(depth, n_dec, d_qk)

    tm_p, tm_s = ROW_TILE, n_dec
    xp = x_prompt.reshape(n_seq * seq_len, d)
    xs = x_sample.reshape(n_dec, d)

    conv_p, c_p, n_p, m_p = [], None, [], []
    conv_s, c_s, n_s, m_s = [], None, [], []
    for l in range(depth):
        last = l == depth - 1
        mix_kw = dict(layer=l, plain_cols=plain_cols, q_cols=d_qk, q_scale=q_scale)

        xp = _ffn(xp, g1, w1_gu, w1_d, g_fin, layer=l, tm=tm_p, final_norm=False)
        ya_in, tail = _conv_proj_seq(xp, gmx, w_conv3, w_conv, layer=l, tm=tm_p, seq_len=seq_len)
        z2, gates = _mix_proj(xp, gmx, w_main, w_tail, bias_if, tm=tm_p, out_dtype=BF16, **mix_kw)
        gates_t = jnp.transpose(gates[:, :2 * N_HEADS]).reshape(2, N_HEADS, -1).transpose(1, 0, 2)
        yb_in, c_p, n_new, m_new = _mlstm_chunked(z2, gates_t, gh, c_p, layer=l, depth=depth, n_seq=n_seq,
                                                  seq_len=seq_len, dk=dk, dv=dv)
        xp = _out_proj(ya_in, yb_in, z2, xp, w_a, w_b, w_o, layer=l, tm=tm_p, gate_col0=gate_col0)
        xp = _ffn(xp, g2, w2_gu, w2_d, g_fin, layer=l, tm=tm_p, final_norm=last)
        conv_p.append(tail[:, 8 - (CONV_W - 1):, :])
        n_p.append(n_new[:, :, 0, :])
        m_p.append(m_new[:, :, 0, 0])

        xs = _ffn(xs, g1, w1_gu, w1_d, g_fin, layer=l, tm=tm_s, final_norm=False)
        ya_in, u_new = _conv_proj_step(xs, gmx, w_conv3, w_conv, state_conv[l, :, 0, :], state_conv[l, :, 1, :],
                                       layer=l)
        z2, gates = _mix_proj(xs, gmx, w_main, w_tail, bias_if, tm=tm_s, out_dtype=F32, **mix_kw)
        yb_in, c_s, n_new, m_new = _mlstm_step(z2, gates, gh, state_mlstm_c, n_state, state_mlstm_m, c_s,
                                               layer=l, dk=dk, dv=dv)
        xs = _out_proj(ya_in, yb_in, z2, xs, w_a, w_b, w_o, layer=l, tm=tm_s, gate_col0=gate_col0)
        xs = _ffn(xs, g2, w2_gu, w2_d, g_fin, layer=l, tm=tm_s, final_norm=last)
        conv_s.append(jnp.stack([state_conv[l, :, 1, :], u_new], axis=1))
        n_s.append(n_new.reshape(n_dec, N_HEADS, dk))
        m_s.append(jnp.transpose(m_new[:, :, 0]))

    return (xp.reshape(n_seq, seq_len, d), xs.reshape(n_dec, 1, d),
            jnp.stack(conv_p), c_p, jnp.stack(n_p), jnp.stack(m_p),
            jnp.stack(conv_s), c_s, jnp.stack(n_s), jnp.stack(m_s))
```

```python
import functools
import math

import jax
import jax.numpy as jnp
from jax import lax
from jax.experimental import pallas as pl
from jax.experimental.pallas import tpu as pltpu

F32 = jnp.float32
BF16 = jnp.bfloat16

N_HEADS = 4
CONV_W = 3
EPS = 1e-6
MLSTM_CHUNK = 256
GATE_LANES = 128
TRANSPOSE_ROWS = 128
MXU_COLS = 256
COL_CHUNK = 2 * MXU_COLS
ROW_TILE = 512
VMEM_LIMIT_BYTES = 56 * 1024 * 1024


def _params(n_axes):
    return pltpu.CompilerParams(dimension_semantics=("arbitrary",) * n_axes,
                                vmem_limit_bytes=VMEM_LIMIT_BYTES)


def _resident(shape, layer):
    zeros = (0,) * (len(shape) - 1)
    return pl.BlockSpec(shape, lambda *_: (layer,) + zeros, pipeline_mode=pl.Buffered(1))


def _chunks(n, step):
    return [(c, min(c + step, n)) for c in range(0, n, step)]


def _rmsnorm(x, g):
    return x * lax.rsqrt(jnp.mean(x * x, axis=-1, keepdims=True) + EPS) * g


def _dot(a, b):
    return jnp.dot(a, b, preferred_element_type=F32)


def _sigmoid(x):
    return 0.5 * jnp.tanh(0.5 * x) + 0.5


def _cast_kernel(x_ref, o_ref):
    o_ref[...] = x_ref[...].astype(o_ref.dtype)


def _cast_cols(w, col0, n_cols, blk):
    depth, d, _ = w.shape
    assert col0 % blk == 0 and n_cols % blk == 0
    return pl.pallas_call(
        _cast_kernel,
        grid=(depth, n_cols // blk),
        in_specs=[pl.BlockSpec((None, d, blk), lambda l, j: (l, 0, col0 // blk + j))],
        out_specs=pl.BlockSpec((None, d, blk), lambda l, j: (l, 0, j)),
        out_shape=jax.ShapeDtypeStruct((depth, d, n_cols), BF16),
        compiler_params=_params(2),
        name="cast_cols",
    )(w)


def _ffn_kernel(x_ref, g_ref, wgu_ref, wd_ref, gf_ref, o_ref, act_scr, *, final_norm):
    d = x_ref.shape[1]
    d_ff = wd_ref.shape[0]
    x = x_ref[...]
    h = _rmsnorm(x, g_ref[...]).astype(BF16)
    for c0, c1 in _chunks(d_ff, COL_CHUNK):
        gate = _dot(h, wgu_ref[:, c0:c1])
        up = _dot(h, wgu_ref[:, d_ff + c0:d_ff + c1])
        act_scr[:, c0:c1] = (gate * _sigmoid(gate) * up).astype(BF16)
    act = act_scr[...]
    for c0, c1 in _chunks(d, COL_CHUNK):
        o_ref[:, c0:c1] = x[:, c0:c1] + 0.5 * _dot(act, wd_ref[:, c0:c1])
    if final_norm:
        o_ref[...] = _rmsnorm(o_ref[...], gf_ref[...])


def _ffn(x, g, w_gu, w_down, g_final, *, layer, tm, final_norm):
    t, d = x.shape
    d_ff = w_down.shape[1]
    kern = functools.partial(_ffn_kernel, final_norm=final_norm)
    return pl.pallas_call(
        kern,
        grid=(t // tm,),
        in_specs=[
            pl.BlockSpec((tm, d), lambda i: (i, 0)),
            _resident((None, 1, d), layer),
            _resident((None, d, 2 * d_ff), layer),
            _resident((None, d_ff, d), layer),
            pl.BlockSpec((1, d), lambda i: (0, 0)),
        ],
        out_specs=pl.BlockSpec((tm, d), lambda i: (i, 0)),
        out_shape=jax.ShapeDtypeStruct((t, d), F32),
        scratch_shapes=[pltpu.VMEM((tm, d_ff), BF16)],
        compiler_params=_params(1),
        name="ffn",
    )(x, g, w_gu, w_down, g_final)


def _conv_seq_kernel(x_ref, g_ref, w3_ref, wconv_ref, ya_ref, tail_ref, ubuf, *, tm, tiles_per_seq):
    i = pl.program_id(0)
    dc = ya_ref.shape[1]

    @pl.when(i % tiles_per_seq == 0)
    def _():
        ubuf[0:8, :] = jnp.zeros((8, dc), F32)

    @pl.when(i % tiles_per_seq != 0)
    def _():
        ubuf[0:8, :] = ubuf[tm:tm + 8, :]

    h = _rmsnorm(x_ref[...], g_ref[...]).astype(BF16)
    w = wconv_ref[...]

    def project(c0, c1):
        return tuple(_dot(h, w3_ref[:, s * dc + c0:s * dc + c1]) for s in range(3))

    chunks = _chunks(dc, MXU_COLS)
    z_next = project(*chunks[0])
    for n, (c0, c1) in enumerate(chunks):
        zb, zc, za = z_next
        if n + 1 < len(chunks):
            z_next = project(*chunks[n + 1])
        u = zc * za
        ubuf[8:tm + 8, c0:c1] = u
        yc = (ubuf[6:tm + 6, c0:c1] * w[0:1, c0:c1] + ubuf[7:tm + 7, c0:c1] * w[1:2, c0:c1]
              + u * w[2:3, c0:c1])
        ya_ref[:, c0:c1] = (zb * yc).astype(BF16)
        tail_ref[0, :, c0:c1] = u[tm - 8:tm, :]


def _conv_proj_seq(x, g, w_conv3, w_conv, *, layer, tm, seq_len):
    t, d = x.shape
    dc = w_conv.shape[2]
    tiles_per_seq = seq_len // tm
    kern = functools.partial(_conv_seq_kernel, tm=tm, tiles_per_seq=tiles_per_seq)
    return pl.pallas_call(
        kern,
        grid=(t // tm,),
        in_specs=[
            pl.BlockSpec((tm, d), lambda i: (i, 0)),
            _resident((None, 1, d), layer),
            _resident((None, d, 3 * dc), layer),
            _resident((None, CONV_W, dc), layer),
        ],
        out_specs=[
            pl.BlockSpec((tm, dc), lambda i: (i, 0)),
            pl.BlockSpec((1, 8, dc), lambda i: (i // tiles_per_seq, 0, 0)),
        ],
        out_shape=[jax.ShapeDtypeStruct((t, dc), BF16),
                   jax.ShapeDtypeStruct((t // seq_len, 8, dc), F32)],
        scratch_shapes=[pltpu.VMEM((tm + 8, dc), F32)],
        compiler_params=_params(1),
        name="conv_proj_seq",
    )(x, g, w_conv3, w_conv)


def _conv_step_kernel(x_ref, g_ref, w3_ref, wconv_ref, b0_ref, b1_ref, ya_ref, u_ref):
    dc = ya_ref.shape[1]
    h = _rmsnorm(x_ref[...], g_ref[...]).astype(BF16)
    zb = _dot(h, w3_ref[:, 0:dc])
    u = _dot(h, w3_ref[:, dc:2 * dc]) * _dot(h, w3_ref[:, 2 * dc:3 * dc])
    w = wconv_ref[...]
    yc = b0_ref[...] * w[0:1, :] + b1_ref[...] * w[1:2, :] + u * w[2:3, :]
    ya_ref[...] = (zb * yc).astype(BF16)
    u_ref[...] = u


def _conv_proj_step(x, g, w_conv3, w_conv, buf0, buf1, *, layer):
    t, d = x.shape
    dc = w_conv.shape[2]
    row = pl.BlockSpec((t, dc), lambda i: (0, 0))
    return pl.pallas_call(
        _conv_step_kernel,
        grid=(1,),
        in_specs=[
            pl.BlockSpec((t, d), lambda i: (0, 0)),
            _resident((None, 1, d), layer),
            _resident((None, d, 3 * dc), layer),
            _resident((None, CONV_W, dc), layer),
            row, row,
        ],
        out_specs=[row, row],
        out_shape=[jax.ShapeDtypeStruct((t, dc), BF16), jax.ShapeDtypeStruct((t, dc), F32)],
        compiler_params=_params(1),
        name="conv_proj_step",
    )(x, g, w_conv3, w_conv, buf0, buf1)


def _log_sigmoid(x):
    return jnp.minimum(x, 0.0) - jnp.log(1.0 + jnp.exp(-jnp.abs(x)))


def _mix_proj_kernel(x_ref, g_ref, wm_ref, wt_ref, bias_ref, z_ref, gate_ref, *, plain_cols, q_cols, q_scale):
    n_main = wm_ref.shape[1]
    n_sig_tail = wt_ref.shape[1] - GATE_LANES
    h = _rmsnorm(x_ref[...], g_ref[...]).astype(BF16)

    gz = _dot(h, wt_ref[:, n_sig_tail:]) + bias_ref[...]
    lane = lax.broadcasted_iota(jnp.int32, gz.shape, 1)
    is_forget = (lane >= N_HEADS) & (lane < 2 * N_HEADS)
    gate_ref[...] = jnp.where(is_forget, _log_sigmoid(gz), gz)

    for c0, c1 in _chunks(n_main + n_sig_tail, COL_CHUNK):
        z = _dot(h, wm_ref[:, c0:c1]) if c0 < n_main else _dot(h, wt_ref[:, c0 - n_main:c1 - n_main])
        if c0 < q_cols:
            z = z * q_scale
        elif c0 >= plain_cols:
            z = _sigmoid(z)
        z_ref[:, c0:c1] = z.astype(z_ref.dtype)


def _mix_proj(x, g, w_main, w_tail, bias_if, *, layer, tm, plain_cols, q_cols, q_scale, out_dtype):
    t, d = x.shape
    n_main, n_tail = w_main.shape[2], w_tail.shape[2]
    n = n_main + n_tail - GATE_LANES
    assert q_cols % COL_CHUNK == 0 and plain_cols % COL_CHUNK == 0 and n_main % COL_CHUNK == 0
    kern = functools.partial(_mix_proj_kernel, plain_cols=plain_cols, q_cols=q_cols, q_scale=q_scale)
    return pl.pallas_call(
        kern,
        grid=(t // tm,),
        in_specs=[
            pl.BlockSpec((tm, d), lambda i: (i, 0)),
            _resident((None, 1, d), layer),
            _resident((None, d, n_main), layer),
            _resident((None, d, n_tail), layer),
            _resident((None, 1, GATE_LANES), layer),
        ],
        out_specs=[
            pl.BlockSpec((tm, n), lambda i: (i, 0)),
            pl.BlockSpec((tm, GATE_LANES), lambda i: (i, 0)),
        ],
        out_shape=[jax.ShapeDtypeStruct((t, n), out_dtype),
                   jax.ShapeDtypeStruct((t, GATE_LANES), F32)],
        compiler_params=_params(1),
        name="mix_proj",
    )(x, g, w_main, w_tail, bias_if)


def _alternate(primary, secondary):
    for _ in primary:
        next(secondary, None)
    for _ in secondary:
        pass


def _mlstm_kernel(*refs, lc, n_chunks, dk, dv, n_alias):
    prompt_in, sample_in = refs[:6], refs[6:13]
    outs = refs[13 + n_alias:]
    yb_ref, c_out, n_out, m_out = outs[0:4]
    c_scr, n_scr, m_scr = outs[8:11]
    c = pl.program_id(1)

    @pl.when(c == 0)
    def _():
        c_scr[...] = jnp.zeros(c_scr.shape, F32)
        n_scr[...] = jnp.zeros(n_scr.shape, F32)
        m_scr[...] = jnp.zeros(m_scr.shape, F32)

    _alternate(_mlstm_chunk_phases(*prompt_in, yb_ref, c_scr, n_scr, m_scr, lc=lc, dk=dk, dv=dv),
               _mlstm_step_phases(*sample_in, *outs[4:8]))

    @pl.when(c == n_chunks - 1)
    def _():
        c_out[...] = c_scr[...]
        for hd in range(N_HEADS):
            n_out[hd] = jnp.broadcast_to(n_scr[hd:hd + 1, :], n_out.shape[1:])
            m_out[hd] = jnp.broadcast_to(m_scr[hd:hd + 1, :], m_out.shape[1:])


def _mlstm_chunk_phases(q_ref, k_ref, v_ref, so_ref, gt_ref, gh_ref, yb_ref, c_scr, n_scr, m_scr, *, lc, dk, dv):
    nh = N_HEADS
    heads = range(nh)
    gates = gt_ref[...]
    m_prev = m_scr[...]

    r_idx = lax.broadcasted_iota(jnp.int32, (lc, lc), 0)
    c_idx = lax.broadcasted_iota(jnp.int32, (lc, lc), 1)
    upper = (r_idx <= c_idx).astype(BF16)
    hi = gates.astype(BF16)
    rem = gates - hi.astype(F32)
    mid = rem.astype(BF16)
    lo = (rem - mid.astype(F32)).astype(BF16)
    b_rows = (_dot(hi, upper) + _dot(mid, upper) + _dot(lo, upper))[nh:, :]
    a_rows = gates[:nh, :] - b_rows
    yield

    lane = lax.broadcasted_iota(jnp.int32, (2 * nh, lc), 1)
    g_rows = jnp.concatenate([a_rows, a_rows], axis=0)
    shift = 1
    while shift < lc:
        g_rows = jnp.maximum(g_rows, jnp.where(lane >= shift, pltpu.roll(g_rows, shift, axis=1), -jnp.inf))
        shift *= 2
        yield
    g_rows = jnp.maximum(g_rows[:nh, :], m_prev)
    g_last = g_rows[:, lc - 1:lc]
    m_scr[...] = b_rows[:, lc - 1:lc] + g_last
    decay = jnp.exp(m_prev - g_last)
    w_rows = jnp.exp(a_rows - g_last)

    pad = jnp.zeros((TRANSPOSE_ROWS - 3 * nh, lc), F32)
    cols = jnp.transpose(jnp.concatenate([g_rows, w_rows, b_rows, pad], axis=0))
    g_col = [cols[:, hd:hd + 1] for hd in heads]
    w_col = [cols[:, nh + hd:nh + hd + 1] for hd in heads]
    b_col = [cols[:, 2 * nh + hd:2 * nh + hd + 1] for hd in heads]
    yield

    causal = c_idx <= r_idx
    n_all = n_scr[...]
    n_bf = n_all.astype(BF16)
    q = [q_ref[:, hd * dk:(hd + 1) * dk] for hd in heads]
    k = [k_ref[:, hd * dk:(hd + 1) * dk] for hd in heads]
    v = [v_ref[:, hd * dv:(hd + 1) * dv] for hd in heads]
    trans_b = (((1,), (1,)), ((), ()))
    trans_a = (((0,), (0,)), ((), ()))

    qk, qc, qn, c_old = [], [], [], []
    for hd in heads:
        qk.append(lax.dot_general(q[hd], k[hd], trans_b, preferred_element_type=F32))
        c_old.append(c_scr[hd])
        qc.append(_dot(q[hd], c_old[hd].astype(BF16)))
        qn.append(lax.dot_general(q[hd], n_bf, trans_b, preferred_element_type=F32)[:, hd:hd + 1])
        yield

    s, sc = [], []
    for hd in heads:
        dmat = jnp.exp(jnp.where(causal, a_rows[hd:hd + 1, :] - g_col[hd], -jnp.inf))
        s.append(qk[hd] * dmat)
        sc.append(jnp.exp(m_prev[hd:hd + 1, :] - g_col[hd]))
        yield

    for hd in heads:
        num = sc[hd] * qc[hd] + _dot(s[hd].astype(BF16), v[hd])
        den = sc[hd] * qn[hd] + jnp.sum(s[hd], axis=1, keepdims=True)
        hraw = num / jnp.maximum(jnp.abs(den), jnp.exp(-(b_col[hd] + g_col[hd])))
        hn = (hraw * lax.rsqrt(jnp.mean(hraw * hraw, axis=1, keepdims=True) + EPS)
              * gh_ref[:, hd * dv:(hd + 1) * dv])
        yb_ref[:, hd * dv:(hd + 1) * dv] = (so_ref[:, hd * dv:(hd + 1) * dv].astype(F32) * hn).astype(BF16)
        yield

    for hd in heads:
        wk = w_col[hd] * k[hd].astype(F32)
        dec = decay[hd:hd + 1, :]
        c_scr[hd] = dec * c_old[hd] + lax.dot_general(wk.astype(BF16), v[hd], trans_a, preferred_element_type=F32)
        n_scr[hd:hd + 1, :] = dec * n_all[hd:hd + 1, :] + jnp.sum(wk, axis=0, keepdims=True)
        yield


def _mlstm_step_phases(q_ref, k_ref, v_ref, so_ref, gate_ref, gh_ref, c_ref, yb_ref, c_out, n_out, m_out):
    n_seq_blk, n_heads = c_ref.shape[0], c_ref.shape[1]
    bb = n_seq_blk * n_heads
    q = q_ref[...]
    k = k_ref[...]
    v = v_ref[...]
    dk = q.shape[1]
    i_pre = gate_ref[:, 0:1]
    log_f = gate_ref[:, 1:2]
    m_old = gate_ref[:, 2:3]
    n_old = gate_ref[:, GATE_LANES:GATE_LANES + dk]

    inter = log_f + m_old
    m_new = jnp.maximum(inter, i_pre)
    s_w = jnp.exp(i_pre - m_new)
    sc = jnp.exp(inter - m_new)
    s = jnp.sum(q * k, axis=1, keepdims=True) * s_w
    den = sc * jnp.sum(q * n_old, axis=1, keepdims=True) + s
    denom = jnp.maximum(jnp.abs(den), jnp.exp(-m_new))

    wk = s_w * k
    pad = jnp.zeros((TRANSPOSE_ROWS - bb, q.shape[1]), F32)
    q_t = jnp.transpose(jnp.concatenate([q, pad], axis=0))
    wk_t = jnp.transpose(jnp.concatenate([wk, pad], axis=0))
    n_out[...] = sc * n_old + wk
    m_out[...] = jnp.broadcast_to(m_new, m_out.shape)
    yield

    qc_rows = []
    for j in range(bb):
        c_old = c_ref[j // n_heads, j % n_heads]
        qc_rows.append(jnp.sum(q_t[:, j:j + 1] * c_old, axis=0, keepdims=True))
        c_out[j // n_heads, j % n_heads] = sc[j:j + 1, :] * c_old + wk_t[:, j:j + 1] * v[j:j + 1, :]
        yield
    qc = jnp.concatenate(qc_rows, axis=0)

    hraw = (sc * qc + s * v) / denom
    hn = hraw * lax.rsqrt(jnp.mean(hraw * hraw, axis=1, keepdims=True) + EPS) * gh_ref[...]
    yb_ref[...] = (so_ref[...] * hn).astype(BF16)
    yield


def _mlstm(z2, gates_t, g_head, z2_s, gates_s, c_state, n_state, m_state, c_acc_p, c_acc_s, *,
           layer, n_seq, seq_len, dk, dv):
    depth, n_dec = c_state.shape[0], c_state.shape[1]
    lc = MLSTM_CHUNK
    n_chunks = seq_len // lc
    n_steps = n_seq * n_chunks
    t = n_seq * seq_len
    d_qk, d_v = N_HEADS * dk, N_HEADS * dv
    assert d_v == 2 * d_qk and n_dec % n_steps == 0
    seq_blk = n_dec // n_steps
    bb = seq_blk * N_HEADS
    assert bb % 8 == 0

    pairs = lambda a, w: a.reshape(n_steps, bb, w)
    sq, sk = pairs(z2_s[:, :d_qk], dk), pairs(z2_s[:, d_qk:2 * d_qk], dk)
    sv, sso = pairs(z2_s[:, 2 * d_qk:2 * d_qk + d_v], dv), pairs(z2_s[:, 2 * d_qk + d_v:2 * d_qk + 2 * d_v], dv)
    scal = jnp.stack([gates_s[:, :N_HEADS], gates_s[:, N_HEADS:2 * N_HEADS], m_state], axis=-1)
    scal = jnp.pad(scal, ((0, 0), (0, 0), (0, GATE_LANES - 3)))
    sg = pairs(jnp.concatenate([scal, n_state], axis=-1), GATE_LANES + dk)
    sgh = jnp.tile(g_head[layer].reshape(N_HEADS, dv), (seq_blk, 1))

    row = lambda b, c: b * n_chunks + c
    step3 = lambda w: pl.BlockSpec((None, bb, w), lambda b, c: (row(b, c), 0, 0))
    c_spec_s = pl.BlockSpec((None, seq_blk, N_HEADS, dk, dv), lambda b, c: (layer, row(b, c), 0, 0, 0))
    in_specs = [
        pl.BlockSpec((lc, d_qk), lambda b, c: (row(b, c), 0)),
        pl.BlockSpec((lc, d_qk), lambda b, c: (row(b, c), 1)),
        pl.BlockSpec((lc, d_v), lambda b, c: (row(b, c), 1)),
        pl.BlockSpec((lc, d_v), lambda b, c: (row(b, c), 2)),
        pl.BlockSpec((2 * N_HEADS, lc), lambda b, c: (0, row(b, c))),
        _resident((None, 1, d_v), layer),
        step3(dk), step3(dk), step3(dv), step3(dv), step3(GATE_LANES + dk),
        pl.BlockSpec((bb, dv), lambda b, c: (0, 0)),
        c_spec_s,
    ]
    args = [z2, z2, z2, z2, gates_t, g_head, sq, sk, sv, sso, sg, sgh, c_state]
    aliases = {}
    if c_acc_p is not None:
        in_specs += [pl.BlockSpec(memory_space=pl.ANY)] * 2
        aliases = {len(args): 1, len(args) + 1: 5}
        args += [c_acc_p, c_acc_s]
    kern = functools.partial(_mlstm_kernel, lc=lc, n_chunks=n_chunks, dk=dk, dv=dv, n_alias=len(aliases))
    out = pl.pallas_call(
        kern,
        grid=(n_seq, n_chunks),
        in_specs=in_specs,
        out_specs=[
            pl.BlockSpec((lc, d_v), lambda b, c: (row(b, c), 0)),
            pl.BlockSpec((None, None, N_HEADS, dk, dv), lambda b, c: (layer, b, 0, 0, 0)),
            pl.BlockSpec((None, N_HEADS, 8, dk), lambda b, c: (b, 0, 0, 0)),
            pl.BlockSpec((None, N_HEADS, 8, 128), lambda b, c: (b, 0, 0, 0)),
            step3(dv), c_spec_s, step3(dk), step3(128),
        ],
        out_shape=[
            jax.ShapeDtypeStruct((t, d_v), BF16),
            jax.ShapeDtypeStruct((depth, n_seq, N_HEADS, dk, dv), F32),
            jax.ShapeDtypeStruct((n_seq, N_HEADS, 8, dk), F32),
            jax.ShapeDtypeStruct((n_seq, N_HEADS, 8, 128), F32),
            jax.ShapeDtypeStruct((n_steps, bb, dv), BF16),
            jax.ShapeDtypeStruct(c_state.shape, F32),
            jax.ShapeDtypeStruct((n_steps, bb, dk), F32),
            jax.ShapeDtypeStruct((n_steps, bb, 128), F32),
        ],
        scratch_shapes=[pltpu.VMEM((N_HEADS, dk, dv), F32), pltpu.VMEM((N_HEADS, dk), F32),
                        pltpu.VMEM((N_HEADS, 1), F32)],
        input_output_aliases=aliases,
        compiler_params=_params(2),
        name="mlstm",
    )(*args)
    yb, c_p, n_p, m_p, yb_s, c_s, n_s, m_s = out
    return (yb, c_p, n_p[:, :, 0, :], m_p[:, :, 0, 0],
            yb_s.reshape(n_dec, d_v), c_s, n_s.reshape(n_dec, N_HEADS, dk), m_s[:, :, 0].reshape(n_dec, N_HEADS))


def _out_proj_kernel(ya_ref, yb_ref, sa_ref, sb_ref, x_ref, wa_ref, wb_ref, wo_ref, o_ref):
    d = o_ref.shape[1]
    ya = _dot(ya_ref[...], wa_ref[...])
    yb = _dot(yb_ref[...], wb_ref[...])
    mix = (sa_ref[...].astype(F32) * ya + sb_ref[...].astype(F32) * yb).astype(BF16)
    for c0, c1 in _chunks(d, COL_CHUNK):
        o_ref[:, c0:c1] = x_ref[:, c0:c1] + _dot(mix, wo_ref[:, c0:c1])


def _out_proj(ya_in, yb_in, z2, x, w_a, w_b, w_o, *, layer, tm, gate_col0):
    t, d = x.shape
    da, db = ya_in.shape[1], yb_in.shape[1]
    ga_blk = gate_col0 // d
    return pl.pallas_call(
        _out_proj_kernel,
        grid=(t // tm,),
        in_specs=[
            pl.BlockSpec((tm, da), lambda i: (i, 0)),
            pl.BlockSpec((tm, db), lambda i: (i, 0)),
            pl.BlockSpec((tm, d), lambda i: (i, ga_blk)),
            pl.BlockSpec((tm, d), lambda i: (i, ga_blk + 1)),
            pl.BlockSpec((tm, d), lambda i: (i, 0)),
            _resident((None, da, d), layer), _resident((None, db, d), layer), _resident((None, d, d), layer),
        ],
        out_specs=pl.BlockSpec((tm, d), lambda i: (i, 0)),
        out_shape=jax.ShapeDtypeStruct((t, d), F32),
        compiler_params=_params(1),
        name="out_proj",
    )(ya_in, yb_in, z2, z2, x, w_a, w_b, w_o)


def kernel(x_prompt, x_sample, state_conv, state_mlstm_c, state_mlstm_n, state_mlstm_m, g_ffn1, w_ffn1_gu, w_ffn1_down, g_mix, w_in, w_conv, b_igate, b_fgate, g_head, w_proj_a, w_proj_b, w_out, g_ffn2, w_ffn2_gu, w_ffn2_down, g_final):
    n_seq, seq_len, d = x_prompt.shape
    n_dec = x_sample.shape[0]
    depth = w_in.shape[0]
    d_conv = w_conv.shape[2]
    d_mlstm = g_head.shape[1]
    dv = d_mlstm // N_HEADS
    dk = state_mlstm_n.shape[3]
    d_qk = N_HEADS * dk
    assert x_sample.shape[1] == 1 and state_conv.shape[2] == CONV_W - 1

    c_q = 3 * d_conv
    c_if = c_q + 2 * d_qk + 2 * d_mlstm
    c_ga = c_if + 2 * N_HEADS
    plain_cols = 2 * d_qk + d_mlstm
    gate_col0 = plain_cols + d_mlstm
    q_scale = float(dk) ** -0.5
    gate_pad = GATE_LANES - 2 * N_HEADS

    w1_gu, w1_d = w_ffn1_gu.astype(BF16), w_ffn1_down.astype(BF16)
    w2_gu, w2_d = w_ffn2_gu.astype(BF16), w_ffn2_down.astype(BF16)
    cast_blk = math.gcd(c_q, c_if - c_q)
    while cast_blk > 2048 and cast_blk % (2 * GATE_LANES) == 0:
        cast_blk //= 2
    w_conv3 = _cast_cols(w_in, 0, c_q, cast_blk)
    w_main = _cast_cols(w_in, c_q, c_if - c_q, cast_blk)
    w_tail = jnp.pad(jnp.concatenate([w_in[:, :, c_ga:], w_in[:, :, c_if:c_ga]], axis=2),
                     ((0, 0), (0, 0), (0, gate_pad))).astype(BF16)
    bias_if = jnp.pad(jnp.concatenate([b_igate, b_fgate], axis=1), ((0, 0), (0, gate_pad)))[:, None, :]
    w_a, w_b, w_o = w_proj_a.astype(BF16), w_proj_b.astype(BF16), w_out.astype(BF16)
    g1, gmx, g2, gh = g_ffn1[:, None, :], g_mix[:, None, :], g_ffn2[:, None, :], g_head[:, None, :]
    g_fin = g_final.reshape(1, -1)

    tm_p, tm_s = ROW_TILE, n_dec
    xp = x_prompt.reshape(n_seq * seq_len, d)
    xs = x_sample.reshape(n_dec, d)

    conv_p, c_p, n_p, m_p = [], None, [], []
    conv_s, c_s, n_s, m_s = [], None, [], []
    for l in range(depth):
        last = l == depth - 1
        mix_kw = dict(layer=l, plain_cols=plain_cols, q_cols=d_qk, q_scale=q_scale)

        xp = _ffn(xp, g1, w1_gu, w1_d, g_fin, layer=l, tm=tm_p, final_norm=False)
        ya_p, tail = _conv_proj_seq(xp, gmx, w_conv3, w_conv, layer=l, tm=tm_p, seq_len=seq_len)
        z2_p, gates_p = _mix_proj(xp, gmx, w_main, w_tail, bias_if, tm=tm_p, out_dtype=BF16, **mix_kw)
        gates_t = jnp.transpose(gates_p[:, :2 * N_HEADS])

        xs = _ffn(xs, g1, w1_gu, w1_d, g_fin, layer=l, tm=tm_s, final_norm=False)
        ya_s, u_new = _conv_proj_step(xs, gmx, w_conv3, w_conv, state_conv[l, :, 0, :], state_conv[l, :, 1, :],
                                      layer=l)
        z2_s, gates_s = _mix_proj(xs, gmx, w_main, w_tail, bias_if, tm=tm_s, out_dtype=F32, **mix_kw)

        yb_p, c_p, n_new_p, m_new_p, yb_s, c_s, n_new_s, m_new_s = _mlstm(
            z2_p, gates_t, gh, z2_s, gates_s, state_mlstm_c, state_mlstm_n[l], state_mlstm_m[l], c_p, c_s,
            layer=l, n_seq=n_seq, seq_len=seq_len, dk=dk, dv=dv)

        xp = _out_proj(ya_p, yb_p, z2_p, xp, w_a, w_b, w_o, layer=l, tm=tm_p, gate_col0=gate_col0)
        xp = _ffn(xp, g2, w2_gu, w2_d, g_fin, layer=l, tm=tm_p, final_norm=last)
        xs = _out_proj(ya_s, yb_s, z2_s, xs, w_a, w_b, w_o, layer=l, tm=tm_s, gate_col0=gate_col0)
        xs = _ffn(xs, g2, w2_gu, w2_d, g_fin, layer=l, tm=tm_s, final_norm=last)

        conv_p.append(tail[:, 8 - (CONV_W - 1):, :])
        n_p.append(n_new_p)
        m_p.append(m_new_p)
        conv_s.append(jnp.stack([state_conv[l, :, 1, :], u_new], axis=1))
        n_s.append(n_new_s)
        m_s.append(m_new_s)

    return (xp.reshape(n_seq, seq_len, d), xs.reshape(n_dec, 1, d),
            jnp.stack(conv_p), c_p, jnp.stack(n_p), jnp.stack(m_p),
            jnp.stack(conv_s), c_s, jnp.stack(n_s), jnp.stack(m_s))
```

```python
import functools
import math

import jax
import jax.numpy as jnp
from jax import lax
from jax.experimental import pallas as pl
from jax.experimental.pallas import tpu as pltpu

F32 = jnp.float32
BF16 = jnp.bfloat16

N_HEADS = 4
CONV_W = 3
EPS = 1e-6
MLSTM_CHUNK = 256
GATE_LANES = 128
TRANSPOSE_ROWS = 128
MXU_COLS = 256
COL_CHUNK = 2 * MXU_COLS
ROW_TILES = (1024, 512, 256, 128)
VMEM_LIMIT_BYTES = 56 * 1024 * 1024
VMEM_WINDOW_BUDGET = 44 * 1024 * 1024


def _row_tile(t, row_bytes, resident_bytes, multiple_of=None):
    for tm in ROW_TILES:
        if t % tm == 0 and (multiple_of is None or multiple_of % tm == 0):
            if resident_bytes + 2 * tm * row_bytes <= VMEM_WINDOW_BUDGET:
                return tm
    raise ValueError(f"no row tile fits {t} rows")


def _params(n_axes):
    return pltpu.CompilerParams(dimension_semantics=("arbitrary",) * n_axes,
                                vmem_limit_bytes=VMEM_LIMIT_BYTES)


def _resident(shape, layer):
    zeros = (0,) * (len(shape) - 1)
    return pl.BlockSpec(shape, lambda *_: (layer,) + zeros, pipeline_mode=pl.Buffered(1))


def _chunks(n, step):
    return [(c, min(c + step, n)) for c in range(0, n, step)]


def _rmsnorm(x, g):
    return x * lax.rsqrt(jnp.mean(x * x, axis=-1, keepdims=True) + EPS) * g


def _dot(a, b):
    return jnp.dot(a, b, preferred_element_type=F32)


def _sigmoid(x):
    return 0.5 * jnp.tanh(0.5 * x) + 0.5


def _dot_t(a, b_t):
    return lax.dot_general(a, b_t, (((1,), (1,)), ((), ())), preferred_element_type=F32)


def _cast_kernel(x_ref, o_ref):
    o_ref[...] = x_ref[...].astype(o_ref.dtype)


def _cast_rows(w_t, row0, n_rows, blk):
    depth, _, d = w_t.shape
    assert row0 % blk == 0 and n_rows % blk == 0
    return pl.pallas_call(
        _cast_kernel,
        grid=(depth, n_rows // blk),
        in_specs=[pl.BlockSpec((None, blk, d), lambda l, j: (l, row0 // blk + j, 0))],
        out_specs=pl.BlockSpec((None, blk, d), lambda l, j: (l, j, 0)),
        out_shape=jax.ShapeDtypeStruct((depth, n_rows, d), BF16),
        compiler_params=_params(2),
        name="cast_rows",
    )(w_t)


def _ffn_kernel(x_ref, g_ref, wgu_ref, wd_ref, gf_ref, o_ref, act_scr, *, final_norm):
    d = x_ref.shape[1]
    d_ff = wd_ref.shape[0]
    x = x_ref[...]
    h = _rmsnorm(x, g_ref[...]).astype(BF16)
    for c0, c1 in _chunks(d_ff, COL_CHUNK):
        gate = _dot(h, wgu_ref[:, c0:c1])
        up = _dot(h, wgu_ref[:, d_ff + c0:d_ff + c1])
        act_scr[:, c0:c1] = (gate * _sigmoid(gate) * up).astype(BF16)
    act = act_scr[...]
    for c0, c1 in _chunks(d, COL_CHUNK):
        o_ref[:, c0:c1] = x[:, c0:c1] + 0.5 * _dot(act, wd_ref[:, c0:c1])
    if final_norm:
        o_ref[...] = _rmsnorm(o_ref[...], gf_ref[...])


def _ffn(x, g, w_gu, w_down, g_final, *, layer, final_norm):
    t, d = x.shape
    d_ff = w_down.shape[1]
    tm = _row_tile(t, 4 * d + 4 * d + d_ff, 2 * 3 * d * d_ff)
    kern = functools.partial(_ffn_kernel, final_norm=final_norm)
    return pl.pallas_call(
        kern,
        grid=(t // tm,),
        in_specs=[
            pl.BlockSpec((tm, d), lambda i: (i, 0)),
            _resident((None, 1, d), layer),
            _resident((None, d, 2 * d_ff), layer),
            _resident((None, d_ff, d), layer),
            pl.BlockSpec((1, d), lambda i: (0, 0)),
        ],
        out_specs=pl.BlockSpec((tm, d), lambda i: (i, 0)),
        out_shape=jax.ShapeDtypeStruct((t, d), F32),
        scratch_shapes=[pltpu.VMEM((tm, d_ff), BF16)],
        compiler_params=_params(1),
        name="ffn",
    )(x, g, w_gu, w_down, g_final)


def _conv_seq_kernel(x_ref, g_ref, w3_ref, wconv_ref, ya_ref, tail_ref, ubuf, *, tm, tiles_per_seq):
    i = pl.program_id(0)
    dc = ya_ref.shape[1]

    @pl.when(i % tiles_per_seq == 0)
    def _():
        ubuf[0:8, :] = jnp.zeros((8, dc), F32)

    @pl.when(i % tiles_per_seq != 0)
    def _():
        ubuf[0:8, :] = ubuf[tm:tm + 8, :]

    h = _rmsnorm(x_ref[...], g_ref[...]).astype(BF16)
    w = wconv_ref[...]

    def project(c0, c1):
        return tuple(_dot_t(h, w3_ref[s * dc + c0:s * dc + c1, :]) for s in range(3))

    chunks = _chunks(dc, MXU_COLS)
    z_next = project(*chunks[0])
    for n, (c0, c1) in enumerate(chunks):
        zb, zc, za = z_next
        if n + 1 < len(chunks):
            z_next = project(*chunks[n + 1])
        u = zc * za
        ubuf[8:tm + 8, c0:c1] = u
        yc = (ubuf[6:tm + 6, c0:c1] * w[0:1, c0:c1] + ubuf[7:tm + 7, c0:c1] * w[1:2, c0:c1]
              + u * w[2:3, c0:c1])
        ya_ref[:, c0:c1] = (zb * yc).astype(BF16)
        tail_ref[0, :, c0:c1] = u[tm - 8:tm, :]


def _conv_proj_seq(x, g, w_conv3, w_conv, *, layer, seq_len):
    t, d = x.shape
    dc = w_conv.shape[2]
    tm = _row_tile(t, 4 * d + 2 * dc + 2 * dc, 2 * 3 * dc * d, multiple_of=seq_len)
    tiles_per_seq = seq_len // tm
    kern = functools.partial(_conv_seq_kernel, tm=tm, tiles_per_seq=tiles_per_seq)
    return pl.pallas_call(
        kern,
        grid=(t // tm,),
        in_specs=[
            pl.BlockSpec((tm, d), lambda i: (i, 0)),
            _resident((None, 1, d), layer),
            _resident((None, 3 * dc, d), layer),
            _resident((None, CONV_W, dc), layer),
        ],
        out_specs=[
            pl.BlockSpec((tm, dc), lambda i: (i, 0)),
            pl.BlockSpec((1, 8, dc), lambda i: (i // tiles_per_seq, 0, 0)),
        ],
        out_shape=[jax.ShapeDtypeStruct((t, dc), BF16),
                   jax.ShapeDtypeStruct((t // seq_len, 8, dc), F32)],
        scratch_shapes=[pltpu.VMEM((tm + 8, dc), F32)],
        compiler_params=_params(1),
        name="conv_proj_seq",
    )(x, g, w_conv3, w_conv)


def _conv_step_kernel(x_ref, g_ref, w3_ref, wconv_ref, b0_ref, b1_ref, ya_ref, u_ref):
    dc = ya_ref.shape[1]
    h = _rmsnorm(x_ref[...], g_ref[...]).astype(BF16)
    zb = _dot_t(h, w3_ref[0:dc, :])
    u = _dot_t(h, w3_ref[dc:2 * dc, :]) * _dot_t(h, w3_ref[2 * dc:3 * dc, :])
    w = wconv_ref[...]
    yc = b0_ref[...] * w[0:1, :] + b1_ref[...] * w[1:2, :] + u * w[2:3, :]
    ya_ref[...] = (zb * yc).astype(BF16)
    u_ref[...] = u


def _conv_proj_step(x, g, w_conv3, w_conv, buf0, buf1, *, layer):
    t, d = x.shape
    dc = w_conv.shape[2]
    row = pl.BlockSpec((t, dc), lambda i: (0, 0))
    return pl.pallas_call(
        _conv_step_kernel,
        grid=(1,),
        in_specs=[
            pl.BlockSpec((t, d), lambda i: (0, 0)),
            _resident((None, 1, d), layer),
            _resident((None, 3 * dc, d), layer),
            _resident((None, CONV_W, dc), layer),
            row, row,
        ],
        out_specs=[row, row],
        out_shape=[jax.ShapeDtypeStruct((t, dc), BF16), jax.ShapeDtypeStruct((t, dc), F32)],
        compiler_params=_params(1),
        name="conv_proj_step",
    )(x, g, w_conv3, w_conv, buf0, buf1)


def _log_sigmoid(x):
    return jnp.minimum(x, 0.0) - jnp.log(1.0 + jnp.exp(-jnp.abs(x)))


def _mix_proj_kernel(x_ref, g_ref, wm_ref, wt_ref, bias_ref, z_ref, gate_ref, *, plain_cols, q_cols, q_scale):
    n_main = wm_ref.shape[0]
    n_sig_tail = wt_ref.shape[0] - GATE_LANES
    h = _rmsnorm(x_ref[...], g_ref[...]).astype(BF16)

    gz = _dot_t(h, wt_ref[n_sig_tail:, :]) + bias_ref[...]
    lane = lax.broadcasted_iota(jnp.int32, gz.shape, 1)
    is_forget = (lane >= N_HEADS) & (lane < 2 * N_HEADS)
    gate_ref[...] = jnp.where(is_forget, _log_sigmoid(gz), gz)

    for c0, c1 in _chunks(n_main + n_sig_tail, COL_CHUNK):
        z = _dot_t(h, wm_ref[c0:c1, :]) if c0 < n_main else _dot_t(h, wt_ref[c0 - n_main:c1 - n_main, :])
        if c0 < q_cols:
            z = z * q_scale
        elif c0 >= plain_cols:
            z = _sigmoid(z)
        z_ref[:, c0:c1] = z.astype(z_ref.dtype)


def _mix_proj(x, g, w_main, w_tail, bias_if, *, layer, plain_cols, q_cols, q_scale, out_dtype):
    t, d = x.shape
    n_main, n_tail = w_main.shape[1], w_tail.shape[1]
    n = n_main + n_tail - GATE_LANES
    assert q_cols % COL_CHUNK == 0 and plain_cols % COL_CHUNK == 0 and n_main % COL_CHUNK == 0
    tm = _row_tile(t, 4 * d + n * jnp.dtype(out_dtype).itemsize + 4 * GATE_LANES, 2 * (n_main + n_tail) * d)
    kern = functools.partial(_mix_proj_kernel, plain_cols=plain_cols, q_cols=q_cols, q_scale=q_scale)
    return pl.pallas_call(
        kern,
        grid=(t // tm,),
        in_specs=[
            pl.BlockSpec((tm, d), lambda i: (i, 0)),
            _resident((None, 1, d), layer),
            _resident((None, n_main, d), layer),
            _resident((None, n_tail, d), layer),
            _resident((None, 1, GATE_LANES), layer),
        ],
        out_specs=[
            pl.BlockSpec((tm, n), lambda i: (i, 0)),
            pl.BlockSpec((tm, GATE_LANES), lambda i: (i, 0)),
        ],
        out_shape=[jax.ShapeDtypeStruct((t, n), out_dtype),
                   jax.ShapeDtypeStruct((t, GATE_LANES), F32)],
        compiler_params=_params(1),
        name="mix_proj",
    )(x, g, w_main, w_tail, bias_if)


def _alternate(primary, secondary):
    for _ in primary:
        next(secondary, None)
    for _ in secondary:
        pass


def _mlstm_kernel(*refs, lc, n_chunks, dk, dv, n_alias):
    prompt_in, sample_in = refs[:6], refs[6:13]
    outs = refs[13 + n_alias:]
    yb_ref, c_out, n_out, m_out = outs[0:4]
    c_scr, n_scr, m_scr = outs[8:11]
    c = pl.program_id(1)

    @pl.when(c == 0)
    def _():
        c_scr[...] = jnp.zeros(c_scr.shape, F32)
        n_scr[...] = jnp.zeros(n_scr.shape, F32)
        m_scr[...] = jnp.zeros(m_scr.shape, F32)

    _alternate(_mlstm_chunk_phases(*prompt_in, yb_ref, c_scr, n_scr, m_scr, lc=lc, dk=dk, dv=dv),
               _mlstm_step_phases(*sample_in, *outs[4:8]))

    @pl.when(c == n_chunks - 1)
    def _():
        c_out[...] = c_scr[...]
        for hd in range(N_HEADS):
            n_out[hd] = jnp.broadcast_to(n_scr[hd:hd + 1, :], n_out.shape[1:])
            m_out[hd] = jnp.broadcast_to(m_scr[hd:hd + 1, :], m_out.shape[1:])


def _mlstm_chunk_phases(q_ref, k_ref, v_ref, so_ref, gt_ref, gh_ref, yb_ref, c_scr, n_scr, m_scr, *, lc, dk, dv):
    nh = N_HEADS
    heads = range(nh)
    gates = gt_ref[...]
    m_prev = m_scr[...]

    r_idx = lax.broadcasted_iota(jnp.int32, (lc, lc), 0)
    c_idx = lax.broadcasted_iota(jnp.int32, (lc, lc), 1)
    upper = (r_idx <= c_idx).astype(BF16)
    hi = gates.astype(BF16)
    rem = gates - hi.astype(F32)
    mid = rem.astype(BF16)
    lo = (rem - mid.astype(F32)).astype(BF16)
    b_rows = (_dot(hi, upper) + _dot(mid, upper) + _dot(lo, upper))[nh:, :]
    a_rows = gates[:nh, :] - b_rows
    yield

    lane = lax.broadcasted_iota(jnp.int32, (2 * nh, lc), 1)
    g_rows = jnp.concatenate([a_rows, a_rows], axis=0)
    shift = 1
    while shift < lc:
        g_rows = jnp.maximum(g_rows, jnp.where(lane >= shift, pltpu.roll(g_rows, shift, axis=1), -jnp.inf))
        shift *= 2
        yield
    g_rows = jnp.maximum(g_rows[:nh, :], m_prev)
    g_last = g_rows[:, lc - 1:lc]
    m_scr[...] = b_rows[:, lc - 1:lc] + g_last
    decay = jnp.exp(m_prev - g_last)
    w_rows = jnp.exp(a_rows - g_last)

    pad = jnp.zeros((TRANSPOSE_ROWS - 3 * nh, lc), F32)
    cols = jnp.transpose(jnp.concatenate([g_rows, w_rows, b_rows, pad], axis=0))
    g_col = [cols[:, hd:hd + 1] for hd in heads]
    w_col = [cols[:, nh + hd:nh + hd + 1] for hd in heads]
    b_col = [cols[:, 2 * nh + hd:2 * nh + hd + 1] for hd in heads]
    yield

    causal = c_idx <= r_idx
    n_all = n_scr[...]
    n_bf = n_all.astype(BF16)
    q = [q_ref[:, hd * dk:(hd + 1) * dk] for hd in heads]
    k = [k_ref[:, hd * dk:(hd + 1) * dk] for hd in heads]
    v = [v_ref[:, hd * dv:(hd + 1) * dv] for hd in heads]
    trans_b = (((1,), (1,)), ((), ()))
    trans_a = (((0,), (0,)), ((), ()))

    qk, qc, qn, c_old = [], [], [], []
    for hd in heads:
        qk.append(lax.dot_general(q[hd], k[hd], trans_b, preferred_element_type=F32))
        c_old.append(c_scr[hd])
        qc.append(_dot(q[hd], c_old[hd].astype(BF16)))
        qn.append(lax.dot_general(q[hd], n_bf, trans_b, preferred_element_type=F32)[:, hd:hd + 1])
        yield

    s, sc = [], []
    for hd in heads:
        dmat = jnp.exp(jnp.where(causal, a_rows[hd:hd + 1, :] - g_col[hd], -jnp.inf))
        s.append(qk[hd] * dmat)
        sc.append(jnp.exp(m_prev[hd:hd + 1, :] - g_col[hd]))
        yield

    for hd in heads:
        num = sc[hd] * qc[hd] + _dot(s[hd].astype(BF16), v[hd])
        den = sc[hd] * qn[hd] + jnp.sum(s[hd], axis=1, keepdims=True)
        hraw = num / jnp.maximum(jnp.abs(den), jnp.exp(-(b_col[hd] + g_col[hd])))
        hn = (hraw * lax.rsqrt(jnp.mean(hraw * hraw, axis=1, keepdims=True) + EPS)
              * gh_ref[:, hd * dv:(hd + 1) * dv])
        yb_ref[:, hd * dv:(hd + 1) * dv] = (so_ref[:, hd * dv:(hd + 1) * dv].astype(F32) * hn).astype(BF16)
        yield

    for hd in heads:
        wk = w_col[hd] * k[hd].astype(F32)
        dec = decay[hd:hd + 1, :]
        c_scr[hd] = dec * c_old[hd] + lax.dot_general(wk.astype(BF16), v[hd], trans_a, preferred_element_type=F32)
        n_scr[hd:hd + 1, :] = dec * n_all[hd:hd + 1, :] + jnp.sum(wk, axis=0, keepdims=True)
        yield


def _mlstm_step_phases(q_ref, k_ref, v_ref, so_ref, gate_ref, gh_ref, c_ref, yb_ref, c_out, n_out, m_out):
    n_seq_blk, n_heads = c_ref.shape[0], c_ref.shape[1]
    bb = n_seq_blk * n_heads
    q = q_ref[...]
    k = k_ref[...]
    v = v_ref[...]
    dk = q.shape[1]
    i_pre = gate_ref[:, 0:1]
    log_f = gate_ref[:, 1:2]
    m_old = gate_ref[:, 2:3]
    n_old = gate_ref[:, GATE_LANES:GATE_LANES + dk]

    inter = log_f + m_old
    m_new = jnp.maximum(inter, i_pre)
    s_w = jnp.exp(i_pre - m_new)
    sc = jnp.exp(inter - m_new)
    s = jnp.sum(q * k, axis=1, keepdims=True) * s_w
    den = sc * jnp.sum(q * n_old, axis=1, keepdims=True) + s
    denom = jnp.maximum(jnp.abs(den), jnp.exp(-m_new))

    wk = s_w * k
    pad = jnp.zeros((TRANSPOSE_ROWS - bb, q.shape[1]), F32)
    q_t = jnp.transpose(jnp.concatenate([q, pad], axis=0))
    wk_t = jnp.transpose(jnp.concatenate([wk, pad], axis=0))
    n_out[...] = sc * n_old + wk
    m_out[...] = jnp.broadcast_to(m_new, m_out.shape)
    yield

    qc_rows = []
    for j in range(bb):
        c_old = c_ref[j // n_heads, j % n_heads]
        qc_rows.append(jnp.sum(q_t[:, j:j + 1] * c_old, axis=0, keepdims=True))
        c_out[j // n_heads, j % n_heads] = sc[j:j + 1, :] * c_old + wk_t[:, j:j + 1] * v[j:j + 1, :]
        yield
    qc = jnp.concatenate(qc_rows, axis=0)

    hraw = (sc * qc + s * v) / denom
    hn = hraw * lax.rsqrt(jnp.mean(hraw * hraw, axis=1, keepdims=True) + EPS) * gh_ref[...]
    yb_ref[...] = (so_ref[...] * hn).astype(BF16)
    yield


def _mlstm(z2, gates_t, g_head, z2_s, gates_s, c_state, n_state, m_state, c_acc_p, c_acc_s, *,
           layer, n_seq, seq_len, dk, dv):
    depth, n_dec = c_state.shape[0], c_state.shape[1]
    lc = MLSTM_CHUNK
    n_chunks = seq_len // lc
    n_steps = n_seq * n_chunks
    t = n_seq * seq_len
    d_qk, d_v = N_HEADS * dk, N_HEADS * dv
    assert d_v == 2 * d_qk and n_dec % n_steps == 0
    seq_blk = n_dec // n_steps
    bb = seq_blk * N_HEADS
    assert bb % 8 == 0

    pairs = lambda a, w: a.reshape(n_steps, bb, w)
    sq, sk = pairs(z2_s[:, :d_qk], dk), pairs(z2_s[:, d_qk:2 * d_qk], dk)
    sv, sso = pairs(z2_s[:, 2 * d_qk:2 * d_qk + d_v], dv), pairs(z2_s[:, 2 * d_qk + d_v:2 * d_qk + 2 * d_v], dv)
    scal = jnp.stack([gates_s[:, :N_HEADS], gates_s[:, N_HEADS:2 * N_HEADS], m_state], axis=-1)
    scal = jnp.pad(scal, ((0, 0), (0, 0), (0, GATE_LANES - 3)))
    sg = pairs(jnp.concatenate([scal, n_state], axis=-1), GATE_LANES + dk)
    sgh = jnp.tile(g_head[layer].reshape(N_HEADS, dv), (seq_blk, 1))

    row = lambda b, c: b * n_chunks + c
    step3 = lambda w: pl.BlockSpec((None, bb, w), lambda b, c: (row(b, c), 0, 0))
    c_spec_s = pl.BlockSpec((None, seq_blk, N_HEADS, dk, dv), lambda b, c: (layer, row(b, c), 0, 0, 0))
    in_specs = [
        pl.BlockSpec((lc, d_qk), lambda b, c: (row(b, c), 0)),
        pl.BlockSpec((lc, d_qk), lambda b, c: (row(b, c), 1)),
        pl.BlockSpec((lc, d_v), lambda b, c: (row(b, c), 1)),
        pl.BlockSpec((lc, d_v), lambda b, c: (row(b, c), 2)),
        pl.BlockSpec((2 * N_HEADS, lc), lambda b, c: (0, row(b, c))),
        _resident((None, 1, d_v), layer),
        step3(dk), step3(dk), step3(dv), step3(dv), step3(GATE_LANES + dk),
        pl.BlockSpec((bb, dv), lambda b, c: (0, 0)),
        c_spec_s,
    ]
    args = [z2, z2, z2, z2, gates_t, g_head, sq, sk, sv, sso, sg, sgh, c_state]
    aliases = {}
    if c_acc_p is not None:
        in_specs += [pl.BlockSpec(memory_space=pl.ANY)] * 2
        aliases = {len(args): 1, len(args) + 1: 5}
        args += [c_acc_p, c_acc_s]
    kern = functools.partial(_mlstm_kernel, lc=lc, n_chunks=n_chunks, dk=dk, dv=dv, n_alias=len(aliases))
    out = pl.pallas_call(
        kern,
        grid=(n_seq, n_chunks),
        in_specs=in_specs,
        out_specs=[
            pl.BlockSpec((lc, d_v), lambda b, c: (row(b, c), 0)),
            pl.BlockSpec((None, None, N_HEADS, dk, dv), lambda b, c: (layer, b, 0, 0, 0)),
            pl.BlockSpec((None, N_HEADS, 8, dk), lambda b, c: (b, 0, 0, 0)),
            pl.BlockSpec((None, N_HEADS, 8, 128), lambda b, c: (b, 0, 0, 0)),
            step3(dv), c_spec_s, step3(dk), step3(128),
        ],
        out_shape=[
            jax.ShapeDtypeStruct((t, d_v), BF16),
            jax.ShapeDtypeStruct((depth, n_seq, N_HEADS, dk, dv), F32),
            jax.ShapeDtypeStruct((n_seq, N_HEADS, 8, dk), F32),
            jax.ShapeDtypeStruct((n_seq, N_HEADS, 8, 128), F32),
            jax.ShapeDtypeStruct((n_steps, bb, dv), BF16),
            jax.ShapeDtypeStruct(c_state.shape, F32),
            jax.ShapeDtypeStruct((n_steps, bb, dk), F32),
            jax.ShapeDtypeStruct((n_steps, bb, 128), F32),
        ],
        scratch_shapes=[pltpu.VMEM((N_HEADS, dk, dv), F32), pltpu.VMEM((N_HEADS, dk), F32),
                        pltpu.VMEM((N_HEADS, 1), F32)],
        input_output_aliases=aliases,
        compiler_params=_params(2),
        name="mlstm",
    )(*args)
    yb, c_p, n_p, m_p, yb_s, c_s, n_s, m_s = out
    return (yb, c_p, n_p[:, :, 0, :], m_p[:, :, 0, 0],
            yb_s.reshape(n_dec, d_v), c_s, n_s.reshape(n_dec, N_HEADS, dk), m_s[:, :, 0].reshape(n_dec, N_HEADS))


def _out_proj_kernel(ya_ref, yb_ref, sa_ref, sb_ref, x_ref, wa_ref, wb_ref, wo_ref, o_ref):
    d = o_ref.shape[1]
    ya = _dot(ya_ref[...], wa_ref[...])
    yb = _dot(yb_ref[...], wb_ref[...])
    mix = (sa_ref[...].astype(F32) * ya + sb_ref[...].astype(F32) * yb).astype(BF16)
    for c0, c1 in _chunks(d, COL_CHUNK):
        o_ref[:, c0:c1] = x_ref[:, c0:c1] + _dot(mix, wo_ref[:, c0:c1])


def _out_proj(ya_in, yb_in, z2, x, w_a, w_b, w_o, *, layer, gate_col0):
    t, d = x.shape
    da, db = ya_in.shape[1], yb_in.shape[1]
    ga_blk = gate_col0 // d
    tm = _row_tile(t, 2 * da + 2 * db + 2 * d * z2.dtype.itemsize + 4 * d + 4 * d, 2 * (da + db + d) * d)
    return pl.pallas_call(
        _out_proj_kernel,
        grid=(t // tm,),
        in_specs=[
            pl.BlockSpec((tm, da), lambda i: (i, 0)),
            pl.BlockSpec((tm, db), lambda i: (i, 0)),
            pl.BlockSpec((tm, d), lambda i: (i, ga_blk)),
            pl.BlockSpec((tm, d), lambda i: (i, ga_blk + 1)),
            pl.BlockSpec((tm, d), lambda i: (i, 0)),
            _resident((None, da, d), layer), _resident((None, db, d), layer), _resident((None, d, d), layer),
        ],
        out_specs=pl.BlockSpec((tm, d), lambda i: (i, 0)),
        out_shape=jax.ShapeDtypeStruct((t, d), F32),
        compiler_params=_params(1),
        name="out_proj",
    )(ya_in, yb_in, z2, z2, x, w_a, w_b, w_o)


def kernel(x_prompt, x_sample, state_conv, state_mlstm_c, state_mlstm_n, state_mlstm_m, g_ffn1, w_ffn1_gu, w_ffn1_down, g_mix, w_in, w_conv, b_igate, b_fgate, g_head, w_proj_a, w_proj_b, w_out, g_ffn2, w_ffn2_gu, w_ffn2_down, g_final):
    n_seq, seq_len, d = x_prompt.shape
    n_dec = x_sample.shape[0]
    depth = w_in.shape[0]
    d_conv = w_conv.shape[2]
    d_mlstm = g_head.shape[1]
    dv = d_mlstm // N_HEADS
    dk = state_mlstm_n.shape[3]
    d_qk = N_HEADS * dk
    assert x_sample.shape[1] == 1 and state_conv.shape[2] == CONV_W - 1

    c_q = 3 * d_conv
    c_if = c_q + 2 * d_qk + 2 * d_mlstm
    c_ga = c_if + 2 * N_HEADS
    plain_cols = 2 * d_qk + d_mlstm
    gate_col0 = plain_cols + d_mlstm
    q_scale = float(dk) ** -0.5
    gate_pad = GATE_LANES - 2 * N_HEADS

    w1_gu, w1_d = w_ffn1_gu.astype(BF16), w_ffn1_down.astype(BF16)
    w2_gu, w2_d = w_ffn2_gu.astype(BF16), w_ffn2_down.astype(BF16)
    cast_blk = math.gcd(c_q, c_if - c_q)
    while cast_blk > 2048 and cast_blk % (2 * GATE_LANES) == 0:
        cast_blk //= 2
    w_in_t = jnp.swapaxes(w_in, 1, 2)
    w_conv3 = _cast_rows(w_in_t, 0, c_q, cast_blk)
    w_main = _cast_rows(w_in_t, c_q, c_if - c_q, cast_blk)
    w_tail = jnp.pad(jnp.concatenate([w_in_t[:, c_ga:, :], w_in_t[:, c_if:c_ga, :]], axis=1),
                     ((0, 0), (0, gate_pad), (0, 0))).astype(BF16)
    bias_if = jnp.pad(jnp.concatenate([b_igate, b_fgate], axis=1), ((0, 0), (0, gate_pad)))[:, None, :]
    w_a, w_b, w_o = w_proj_a.astype(BF16), w_proj_b.astype(BF16), w_out.astype(BF16)
    g1, gmx, g2, gh = g_ffn1[:, None, :], g_mix[:, None, :], g_ffn2[:, None, :], g_head[:, None, :]
    g_fin = g_final.reshape(1, -1)

    xp = x_prompt.reshape(n_seq * seq_len, d)
    xs = x_sample.reshape(n_dec, d)

    conv_p, c_p, n_p, m_p = [], None, [], []
    conv_s, c_s, n_s, m_s = [], None, [], []
    for l in range(depth):
        last = l == depth - 1
        mix_kw = dict(layer=l, plain_cols=plain_cols, q_cols=d_qk, q_scale=q_scale)

        xp = _ffn(xp, g1, w1_gu, w1_d, g_fin, layer=l, final_norm=False)
        ya_p, tail = _conv_proj_seq(xp, gmx, w_conv3, w_conv, layer=l, seq_len=seq_len)
        z2_p, gates_p = _mix_proj(xp, gmx, w_main, w_tail, bias_if, out_dtype=BF16, **mix_kw)
        gates_t = jnp.transpose(gates_p[:, :2 * N_HEADS])

        xs = _ffn(xs, g1, w1_gu, w1_d, g_fin, layer=l, final_norm=False)
        ya_s, u_new = _conv_proj_step(xs, gmx, w_conv3, w_conv, state_conv[l, :, 0, :], state_conv[l, :, 1, :],
                                      layer=l)
        z2_s, gates_s = _mix_proj(xs, gmx, w_main, w_tail, bias_if, out_dtype=F32, **mix_kw)

        yb_p, c_p, n_new_p, m_new_p, yb_s, c_s, n_new_s, m_new_s = _mlstm(
            z2_p, gates_t, gh, z2_s, gates_s, state_mlstm_c, state_mlstm_n[l], state_mlstm_m[l], c_p, c_s,
            layer=l, n_seq=n_seq, seq_len=seq_len, dk=dk, dv=dv)

        xp = _out_proj(ya_p, yb_p, z2_p, xp, w_a, w_b, w_o, layer=l, gate_col0=gate_col0)
        xp = _ffn(xp, g2, w2_gu, w2_d, g_fin, layer=l, final_norm=last)
        xs = _out_proj(ya_s, yb_s, z2_s, xs, w_a, w_b, w_o, layer=l, gate_col0=gate_col0)
        xs = _ffn(xs, g2, w2_gu, w2_d, g_fin, layer=l, final_norm=last)

        conv_p.append(tail[:, 8 - (CONV_W - 1):, :])
        n_p.append(n_new_p)
        m_p.append(m_new_p)
        conv_s.append(jnp.stack([state_conv[l, :, 1, :], u_new], axis=1))
        n_s.append(n_new_s)
        m_s.append(m_new_s)

    return (xp.reshape(n_seq, seq_len, d), xs.reshape(n_dec, 1, d),
            jnp.stack(conv_p), c_p, jnp.stack(n_p), jnp.stack(m_p),
            jnp.stack(conv_s), c_s, jnp.stack(n_s), jnp.stack(m_s))
```

```python
import functools
import math
from typing import NamedTuple, Optional

import jax
import jax.numpy as jnp
from jax import lax
from jax.experimental import pallas as pl
from jax.experimental.pallas import tpu as pltpu

F32 = jnp.float32
BF16 = jnp.bfloat16

N_HEADS = 4
CONV_W = 3
EPS = 1e-6
MLSTM_CHUNK = 256
GATE_LANES = 128
BF16_ROWS = 16
TAIL_SIG_ROW0 = BF16_ROWS * (-(-2 * N_HEADS // BF16_ROWS))
GATE_COL0 = TAIL_SIG_ROW0 - 2 * N_HEADS
TRANSPOSE_ROWS = 128
MXU_COLS = 256
COL_CHUNK = 2 * MXU_COLS
ROW_TILES = (1024, 512, 256, 128)
VMEM_LIMIT_BYTES = 56 * 1024 * 1024
VMEM_WINDOW_BUDGET = 44 * 1024 * 1024


def _row_tile(t, row_bytes, resident_bytes, multiple_of=None):
    for tm in ROW_TILES:
        if t % tm == 0 and (multiple_of is None or multiple_of % tm == 0):
            if resident_bytes + 2 * tm * row_bytes <= VMEM_WINDOW_BUDGET:
                return tm
    raise ValueError(f"no row tile fits {t} rows")


def _params(n_axes):
    return pltpu.CompilerParams(dimension_semantics=("arbitrary",) * n_axes,
                                vmem_limit_bytes=VMEM_LIMIT_BYTES)


def _resident(shape, layer):
    zeros = (0,) * (len(shape) - 1)
    return pl.BlockSpec(shape, lambda *_: (layer,) + zeros, pipeline_mode=pl.Buffered(1))


def _chunks(n, step):
    return [(c, min(c + step, n)) for c in range(0, n, step)]


def _rmsnorm(x, g):
    return x * lax.rsqrt(jnp.mean(x * x, axis=-1, keepdims=True) + EPS) * g


def _dot(a, b):
    return jnp.dot(a, b, preferred_element_type=F32)


def _sigmoid(x):
    return 0.5 * jnp.tanh(0.5 * x) + 0.5


def _dot_t(a, b_t):
    return lax.dot_general(a, b_t, (((1,), (1,)), ((), ())), preferred_element_type=F32)


def _cast_kernel(x_ref, o_ref):
    o_ref[...] = x_ref[...].astype(o_ref.dtype)


def _cast_rows(w_t, row0, n_rows, blk):
    depth, _, d = w_t.shape
    assert row0 % blk == 0 and n_rows % blk == 0
    return pl.pallas_call(
        _cast_kernel,
        grid=(depth, n_rows // blk),
        in_specs=[pl.BlockSpec((None, blk, d), lambda l, j: (l, row0 // blk + j, 0))],
        out_specs=pl.BlockSpec((None, blk, d), lambda l, j: (l, j, 0)),
        out_shape=jax.ShapeDtypeStruct((depth, n_rows, d), BF16),
        compiler_params=_params(2),
        name="cast_rows",
    )(w_t)


def _alternate(primary, secondary):
    for _ in primary:
        next(secondary, None)
    for _ in secondary:
        pass


def _ffn_phases(x_ref, g_ref, wgu_ref, wd_ref, gf_ref, o_ref, act_scr, *, final_norm):
    d = x_ref.shape[1]
    d_ff = wd_ref.shape[0]
    x = x_ref[...]
    h = _rmsnorm(x, g_ref[...]).astype(BF16)
    yield
    for c0, c1 in _chunks(d_ff, COL_CHUNK):
        gate = _dot(h, wgu_ref[:, c0:c1])
        up = _dot(h, wgu_ref[:, d_ff + c0:d_ff + c1])
        act_scr[:, c0:c1] = (gate * _sigmoid(gate) * up).astype(BF16)
        yield
    act = act_scr[...]
    for c0, c1 in _chunks(d, COL_CHUNK):
        o_ref[:, c0:c1] = x[:, c0:c1] + 0.5 * _dot(act, wd_ref[:, c0:c1])
        yield
    if final_norm:
        o_ref[...] = _rmsnorm(o_ref[...], gf_ref[...])


def _ffn_kernel(*refs, final_norm, n_state_in, n_alias):
    ffn_in, state_in = refs[:5], refs[5:5 + n_state_in]
    outs = refs[5 + n_state_in + n_alias:]
    ffn = _ffn_phases(*ffn_in, outs[0], outs[-1], final_norm=final_norm)
    _alternate(ffn, _mlstm_step_phases(*state_in, *outs[1:-1]) if n_state_in else iter(()))


class _SampleState(NamedTuple):
    q: jax.Array
    k: jax.Array
    v: jax.Array
    so: jax.Array
    scal: jax.Array
    g_head: jax.Array
    c_state: jax.Array
    c_acc: Optional[jax.Array]
    layer: int
    slab0: int


def _ffn_tile(t, d, d_ff, state_elems):
    return _row_tile(t, 4 * d + 4 * d + d_ff + 4 * state_elems // t, 2 * 3 * d * d_ff)


def _ffn(x, g, w_gu, w_down, g_final, *, layer, final_norm, state=None):
    t, d = x.shape
    d_ff = w_down.shape[1]
    tm = _ffn_tile(t, d, d_ff, 0 if state is None else math.prod(state.c_state.shape[1:]))
    n_steps = t // tm
    in_specs = [
        pl.BlockSpec((tm, d), lambda i: (i, 0)),
        _resident((None, 1, d), layer),
        _resident((None, d, 2 * d_ff), layer),
        _resident((None, d_ff, d), layer),
        pl.BlockSpec((1, d), lambda i: (0, 0)),
    ]
    args = [x, g, w_gu, w_down, g_final]
    out_specs = [pl.BlockSpec((tm, d), lambda i: (i, 0))]
    out_shape = [jax.ShapeDtypeStruct((t, d), F32)]
    aliases = {}
    n_state_in = 0
    if state is not None:
        _, bb, dk = state.q.shape
        dv = state.v.shape[2]
        n_heads = state.c_state.shape[2]
        slab = lambda w: pl.BlockSpec((None, bb, w), lambda i: (state.slab0 + i, 0, 0))
        c_spec = pl.BlockSpec((None, bb // n_heads, n_heads, dk, dv),
                              lambda i: (state.layer, state.slab0 + i, 0, 0, 0))
        in_specs += [slab(dk), slab(dk), slab(dv), slab(dv), slab(state.scal.shape[2]),
                     pl.BlockSpec((bb, dv), lambda i: (0, 0)), c_spec]
        args += [state.q, state.k, state.v, state.so, state.scal, state.g_head, state.c_state]
        n_state_in = 7
        if state.c_acc is not None:
            in_specs.append(pl.BlockSpec(memory_space=pl.ANY))
            aliases = {len(args): 2}
            args.append(state.c_acc)
        out_slab = lambda w: pl.BlockSpec((None, bb, w), lambda i: (i, 0, 0))
        out_specs += [out_slab(dv), c_spec, out_slab(dk), out_slab(GATE_LANES)]
        out_shape += [jax.ShapeDtypeStruct((n_steps, bb, dv), BF16),
                      jax.ShapeDtypeStruct(state.c_state.shape, F32),
                      jax.ShapeDtypeStruct((n_steps, bb, dk), F32),
                      jax.ShapeDtypeStruct((n_steps, bb, GATE_LANES), F32)]
    kern = functools.partial(_ffn_kernel, final_norm=final_norm, n_state_in=n_state_in, n_alias=len(aliases))
    out = pl.pallas_call(
        kern,
        grid=(n_steps,),
        in_specs=in_specs,
        out_specs=out_specs,
        out_shape=out_shape,
        scratch_shapes=[pltpu.VMEM((tm, d_ff), BF16)],
        input_output_aliases=aliases,
        compiler_params=_params(1),
        name="ffn",
    )(*args)
    return out[0] if state is None else out


def _conv_seq_kernel(x_ref, g_ref, w3_ref, wconv_ref, ya_ref, tail_ref, ubuf, *, tm, tiles_per_seq):
    i = pl.program_id(0)
    dc = ya_ref.shape[1]

    @pl.when(i % tiles_per_seq == 0)
    def _():
        ubuf[0:8, :] = jnp.zeros((8, dc), F32)

    @pl.when(i % tiles_per_seq != 0)
    def _():
        ubuf[0:8, :] = ubuf[tm:tm + 8, :]

    h = _rmsnorm(x_ref[...], g_ref[...]).astype(BF16)
    w = wconv_ref[...]

    def project(c0, c1):
        return tuple(_dot_t(h, w3_ref[s * dc + c0:s * dc + c1, :]) for s in range(3))

    chunks = _chunks(dc, MXU_COLS)
    z_next = project(*chunks[0])
    for n, (c0, c1) in enumerate(chunks):
        zb, zc, za = z_next
        if n + 1 < len(chunks):
            z_next = project(*chunks[n + 1])
        u = zc * za
        ubuf[8:tm + 8, c0:c1] = u
        yc = (ubuf[6:tm + 6, c0:c1] * w[0:1, c0:c1] + ubuf[7:tm + 7, c0:c1] * w[1:2, c0:c1]
              + u * w[2:3, c0:c1])
        ya_ref[:, c0:c1] = (zb * yc).astype(BF16)
        tail_ref[0, :, c0:c1] = u[tm - 8:tm, :]


def _conv_proj_seq(x, g, w_conv3, w_conv, *, layer, seq_len):
    t, d = x.shape
    dc = w_conv.shape[2]
    tm = _row_tile(t, 4 * d + 2 * dc + 2 * dc, 2 * 3 * dc * d, multiple_of=seq_len)
    tiles_per_seq = seq_len // tm
    kern = functools.partial(_conv_seq_kernel, tm=tm, tiles_per_seq=tiles_per_seq)
    return pl.pallas_call(
        kern,
        grid=(t // tm,),
        in_specs=[
            pl.BlockSpec((tm, d), lambda i: (i, 0)),
            _resident((None, 1, d), layer),
            _resident((None, 3 * dc, d), layer),
            _resident((None, CONV_W, dc), layer),
        ],
        out_specs=[
            pl.BlockSpec((tm, dc), lambda i: (i, 0)),
            pl.BlockSpec((1, 8, dc), lambda i: (i // tiles_per_seq, 0, 0)),
        ],
        out_shape=[jax.ShapeDtypeStruct((t, dc), BF16),
                   jax.ShapeDtypeStruct((t // seq_len, 8, dc), F32)],
        scratch_shapes=[pltpu.VMEM((tm + 8, dc), F32)],
        compiler_params=_params(1),
        name="conv_proj_seq",
    )(x, g, w_conv3, w_conv)


def _conv_step_kernel(x_ref, g_ref, w3_ref, wconv_ref, b0_ref, b1_ref, ya_ref, u_ref):
    dc = ya_ref.shape[1]
    h = _rmsnorm(x_ref[...], g_ref[...]).astype(BF16)
    zb = _dot_t(h, w3_ref[0:dc, :])
    u = _dot_t(h, w3_ref[dc:2 * dc, :]) * _dot_t(h, w3_ref[2 * dc:3 * dc, :])
    w = wconv_ref[...]
    yc = b0_ref[...] * w[0:1, :] + b1_ref[...] * w[1:2, :] + u * w[2:3, :]
    ya_ref[...] = (zb * yc).astype(BF16)
    u_ref[...] = u


def _conv_proj_step(x, g, w_conv3, w_conv, buf0, buf1, *, layer):
    t, d = x.shape
    dc = w_conv.shape[2]
    row = pl.BlockSpec((t, dc), lambda i: (0, 0))
    return pl.pallas_call(
        _conv_step_kernel,
        grid=(1,),
        in_specs=[
            pl.BlockSpec((t, d), lambda i: (0, 0)),
            _resident((None, 1, d), layer),
            _resident((None, 3 * dc, d), layer),
            _resident((None, CONV_W, dc), layer),
            row, row,
        ],
        out_specs=[row, row],
        out_shape=[jax.ShapeDtypeStruct((t, dc), BF16), jax.ShapeDtypeStruct((t, dc), F32)],
        compiler_params=_params(1),
        name="conv_proj_step",
    )(x, g, w_conv3, w_conv, buf0, buf1)


def _log_sigmoid(x):
    return jnp.minimum(x, 0.0) - jnp.log(1.0 + jnp.exp(-jnp.abs(x)))


def _mix_proj_kernel(x_ref, g_ref, wm_ref, wt_ref, bias_ref, z_ref, gate_ref, *, plain_cols, q_cols, q_scale):
    n_main = wm_ref.shape[0]
    n_sig_tail = wt_ref.shape[0] - TAIL_SIG_ROW0
    h = _rmsnorm(x_ref[...], g_ref[...]).astype(BF16)

    gz = _dot_t(h, wt_ref[0:GATE_LANES, :]) + bias_ref[...]
    lane = lax.broadcasted_iota(jnp.int32, gz.shape, 1)
    is_forget = (lane >= GATE_COL0 + N_HEADS) & (lane < GATE_COL0 + 2 * N_HEADS)
    gate_ref[...] = jnp.where(is_forget, _log_sigmoid(gz), gz)

    for c0, c1 in _chunks(n_main + n_sig_tail, COL_CHUNK):
        t0 = TAIL_SIG_ROW0 - n_main
        z = _dot_t(h, wm_ref[c0:c1, :]) if c0 < n_main else _dot_t(h, wt_ref[t0 + c0:t0 + c1, :])
        if c0 < q_cols:
            z = z * q_scale
        elif c0 >= plain_cols:
            z = _sigmoid(z)
        z_ref[:, c0:c1] = z.astype(z_ref.dtype)


def _mix_proj(x, g, w_main, w_tail, bias_if, *, layer, plain_cols, q_cols, q_scale, out_dtype):
    t, d = x.shape
    n_main, n_tail = w_main.shape[1], w_tail.shape[1]
    n = n_main + n_tail - TAIL_SIG_ROW0
    assert q_cols % COL_CHUNK == 0 and plain_cols % COL_CHUNK == 0 and n_main % COL_CHUNK == 0
    tm = _row_tile(t, 4 * d + n * jnp.dtype(out_dtype).itemsize + 4 * GATE_LANES, 2 * (n_main + n_tail) * d)
    kern = functools.partial(_mix_proj_kernel, plain_cols=plain_cols, q_cols=q_cols, q_scale=q_scale)
    return pl.pallas_call(
        kern,
        grid=(t // tm,),
        in_specs=[
            pl.BlockSpec((tm, d), lambda i: (i, 0)),
            _resident((None, 1, d), layer),
            _resident((None, n_main, d), layer),
            _resident((None, n_tail, d), layer),
            _resident((None, 1, GATE_LANES), layer),
        ],
        out_specs=[
            pl.BlockSpec((tm, n), lambda i: (i, 0)),
            pl.BlockSpec((tm, GATE_LANES), lambda i: (i, 0)),
        ],
        out_shape=[jax.ShapeDtypeStruct((t, n), out_dtype),
                   jax.ShapeDtypeStruct((t, GATE_LANES), F32)],
        compiler_params=_params(1),
        name="mix_proj",
    )(x, g, w_main, w_tail, bias_if)


def _mlstm_kernel(*refs, lc, n_chunks, dk, dv, n_alias):
    prompt_in = refs[:6]
    yb_ref, c_out, n_out, m_out, c_scr, n_scr, m_scr = refs[6 + n_alias:]
    c = pl.program_id(1)

    @pl.when(c == 0)
    def _():
        c_scr[...] = jnp.zeros(c_scr.shape, F32)
        n_scr[...] = jnp.zeros(n_scr.shape, F32)
        m_scr[...] = jnp.zeros(m_scr.shape, F32)

    for _ in _mlstm_chunk_phases(*prompt_in, yb_ref, c_scr, n_scr, m_scr, lc=lc, dk=dk, dv=dv):
        pass

    @pl.when(c == n_chunks - 1)
    def _():
        c_out[...] = c_scr[...]
        for hd in range(N_HEADS):
            n_out[hd] = jnp.broadcast_to(n_scr[hd:hd + 1, :], n_out.shape[1:])
            m_out[hd] = jnp.broadcast_to(m_scr[hd:hd + 1, :], m_out.shape[1:])


def _mlstm_chunk_phases(q_ref, k_ref, v_ref, so_ref, gt_ref, gh_ref, yb_ref, c_scr, n_scr, m_scr, *, lc, dk, dv):
    nh = N_HEADS
    heads = range(nh)
    gates = gt_ref[...]
    m_prev = m_scr[...]

    r_idx = lax.broadcasted_iota(jnp.int32, (lc, lc), 0)
    c_idx = lax.broadcasted_iota(jnp.int32, (lc, lc), 1)
    upper = (r_idx <= c_idx).astype(BF16)
    hi = gates.astype(BF16)
    rem = gates - hi.astype(F32)
    mid = rem.astype(BF16)
    lo = (rem - mid.astype(F32)).astype(BF16)
    b_rows = (_dot(hi, upper) + _dot(mid, upper) + _dot(lo, upper))[nh:, :]
    a_rows = gates[:nh, :] - b_rows
    yield

    lane = lax.broadcasted_iota(jnp.int32, (2 * nh, lc), 1)
    g_rows = jnp.concatenate([a_rows, a_rows], axis=0)
    shift = 1
    while shift < lc:
        g_rows = jnp.maximum(g_rows, jnp.where(lane >= shift, pltpu.roll(g_rows, shift, axis=1), -jnp.inf))
        shift *= 2
        yield
    g_rows = jnp.maximum(g_rows[:nh, :], m_prev)
    g_last = g_rows[:, lc - 1:lc]
    m_scr[...] = b_rows[:, lc - 1:lc] + g_last
    decay = jnp.exp(m_prev - g_last)
    w_rows = jnp.exp(a_rows - g_last)

    pad = jnp.zeros((TRANSPOSE_ROWS - 3 * nh, lc), F32)
    cols = jnp.transpose(jnp.concatenate([g_rows, w_rows, b_rows, pad], axis=0))
    g_col = [cols[:, hd:hd + 1] for hd in heads]
    w_col = [cols[:, nh + hd:nh + hd + 1] for hd in heads]
    b_col = [cols[:, 2 * nh + hd:2 * nh + hd + 1] for hd in heads]
    yield

    causal = c_idx <= r_idx
    n_all = n_scr[...]
    n_bf = n_all.astype(BF16)
    q = [q_ref[:, hd * dk:(hd + 1) * dk] for hd in heads]
    k = [k_ref[:, hd * dk:(hd + 1) * dk] for hd in heads]
    v = [v_ref[:, hd * dv:(hd + 1) * dv] for hd in heads]
    trans_b = (((1,), (1,)), ((), ()))
    trans_a = (((0,), (0,)), ((), ()))

    qk, qc, qn, c_old = [], [], [], []
    for hd in heads:
        qk.append(lax.dot_general(q[hd], k[hd], trans_b, preferred_element_type=F32))
        c_old.append(c_scr[hd])
        qc.append(_dot(q[hd], c_old[hd].astype(BF16)))
        qn.append(lax.dot_general(q[hd], n_bf, trans_b, preferred_element_type=F32)[:, hd:hd + 1])
        yield

    s, sc = [], []
    for hd in heads:
        dmat = jnp.exp(jnp.where(causal, a_rows[hd:hd + 1, :] - g_col[hd], -jnp.inf))
        s.append(qk[hd] * dmat)
        sc.append(jnp.exp(m_prev[hd:hd + 1, :] - g_col[hd]))
        yield

    for hd in heads:
        num = sc[hd] * qc[hd] + _dot(s[hd].astype(BF16), v[hd])
        den = sc[hd] * qn[hd] + jnp.sum(s[hd], axis=1, keepdims=True)
        hraw = num / jnp.maximum(jnp.abs(den), jnp.exp(-(b_col[hd] + g_col[hd])))
        hn = (hraw * lax.rsqrt(jnp.mean(hraw * hraw, axis=1, keepdims=True) + EPS)
              * gh_ref[:, hd * dv:(hd + 1) * dv])
        yb_ref[:, hd * dv:(hd + 1) * dv] = (so_ref[:, hd * dv:(hd + 1) * dv].astype(F32) * hn).astype(BF16)
        yield

    for hd in heads:
        wk = w_col[hd] * k[hd].astype(F32)
        dec = decay[hd:hd + 1, :]
        c_scr[hd] = dec * c_old[hd] + lax.dot_general(wk.astype(BF16), v[hd], trans_a, preferred_element_type=F32)
        n_scr[hd:hd + 1, :] = dec * n_all[hd:hd + 1, :] + jnp.sum(wk, axis=0, keepdims=True)
        yield


def _mlstm_step_phases(q_ref, k_ref, v_ref, so_ref, gate_ref, gh_ref, c_ref, yb_ref, c_out, n_out, m_out):
    n_seq_blk, n_heads = c_ref.shape[0], c_ref.shape[1]
    bb = n_seq_blk * n_heads
    q = q_ref[...]
    k = k_ref[...]
    v = v_ref[...]
    dk = q.shape[1]
    i_pre = gate_ref[:, 0:1]
    log_f = gate_ref[:, 1:2]
    m_old = gate_ref[:, 2:3]
    n_old = gate_ref[:, GATE_LANES:GATE_LANES + dk]

    inter = log_f + m_old
    m_new = jnp.maximum(inter, i_pre)
    s_w = jnp.exp(i_pre - m_new)
    sc = jnp.exp(inter - m_new)
    s = jnp.sum(q * k, axis=1, keepdims=True) * s_w
    den = sc * jnp.sum(q * n_old, axis=1, keepdims=True) + s
    denom = jnp.maximum(jnp.abs(den), jnp.exp(-m_new))

    wk = s_w * k
    pad = jnp.zeros((TRANSPOSE_ROWS - bb, q.shape[1]), F32)
    q_t = jnp.transpose(jnp.concatenate([q, pad], axis=0))
    wk_t = jnp.transpose(jnp.concatenate([wk, pad], axis=0))
    n_out[...] = sc * n_old + wk
    m_out[...] = jnp.broadcast_to(m_new, m_out.shape)
    sc_rows = jnp.broadcast_to(sc, v.shape)
    yield

    qc_rows = []
    for j in range(bb):
        c_old = c_ref[j // n_heads, j % n_heads]
        qc_rows.append(jnp.sum(q_t[:, j:j + 1] * c_old, axis=0, keepdims=True))
        c_out[j // n_heads, j % n_heads] = sc_rows[j:j + 1, :] * c_old + wk_t[:, j:j + 1] * v[j:j + 1, :]
        yield
    qc = jnp.concatenate(qc_rows, axis=0)

    hraw = (sc * qc + s * v) / denom
    hn = hraw * lax.rsqrt(jnp.mean(hraw * hraw, axis=1, keepdims=True) + EPS) * gh_ref[...]
    yb_ref[...] = (so_ref[...] * hn).astype(BF16)
    yield


def _sample_pairs(z2_s, gates_s, n_state, m_state, g_head, *, n_slabs, dk, dv):
    n_dec = z2_s.shape[0]
    d_qk, d_v = N_HEADS * dk, N_HEADS * dv
    bb = n_dec * N_HEADS // n_slabs
    assert bb * n_slabs == n_dec * N_HEADS and bb % (8 * N_HEADS // math.gcd(8, N_HEADS)) == 0
    pairs = lambda a, w: a.reshape(n_slabs, bb, w)
    scal = jnp.stack([gates_s[:, :N_HEADS], gates_s[:, N_HEADS:], m_state], axis=-1)
    scal = jnp.pad(scal, ((0, 0), (0, 0), (0, GATE_LANES - 3)))
    return dict(
        q=pairs(z2_s[:, :d_qk], dk), k=pairs(z2_s[:, d_qk:2 * d_qk], dk),
        v=pairs(z2_s[:, 2 * d_qk:2 * d_qk + d_v], dv), so=pairs(z2_s[:, 2 * d_qk + d_v:2 * d_qk + 2 * d_v], dv),
        scal=pairs(jnp.concatenate([scal, n_state], axis=-1), GATE_LANES + dk),
        g_head=jnp.tile(g_head.reshape(N_HEADS, dv), (bb // N_HEADS, 1)))


def _mlstm(z2, gates_t, g_head, c_acc, *, layer, depth, n_seq, seq_len, dk, dv):
    lc = MLSTM_CHUNK
    n_chunks = seq_len // lc
    t = n_seq * seq_len
    d_qk, d_v = N_HEADS * dk, N_HEADS * dv
    assert d_v == 2 * d_qk
    row = lambda b, c: b * n_chunks + c
    in_specs = [
        pl.BlockSpec((lc, d_qk), lambda b, c: (row(b, c), 0)),
        pl.BlockSpec((lc, d_qk), lambda b, c: (row(b, c), 1)),
        pl.BlockSpec((lc, d_v), lambda b, c: (row(b, c), 1)),
        pl.BlockSpec((lc, d_v), lambda b, c: (row(b, c), 2)),
        pl.BlockSpec((2 * N_HEADS, lc), lambda b, c: (0, row(b, c))),
        _resident((None, 1, d_v), layer),
    ]
    args = [z2, z2, z2, z2, gates_t, g_head]
    aliases = {}
    if c_acc is not None:
        in_specs.append(pl.BlockSpec(memory_space=pl.ANY))
        aliases = {len(args): 1}
        args.append(c_acc)
    kern = functools.partial(_mlstm_kernel, lc=lc, n_chunks=n_chunks, dk=dk, dv=dv, n_alias=len(aliases))
    yb, c_p, n_p, m_p = pl.pallas_call(
        kern,
        grid=(n_seq, n_chunks),
        in_specs=in_specs,
        out_specs=[
            pl.BlockSpec((lc, d_v), lambda b, c: (row(b, c), 0)),
            pl.BlockSpec((None, None, N_HEADS, dk, dv), lambda b, c: (layer, b, 0, 0, 0)),
            pl.BlockSpec((None, N_HEADS, 8, dk), lambda b, c: (b, 0, 0, 0)),
            pl.BlockSpec((None, N_HEADS, 8, 128), lambda b, c: (b, 0, 0, 0)),
        ],
        out_shape=[
            jax.ShapeDtypeStruct((t, d_v), BF16),
            jax.ShapeDtypeStruct((depth, n_seq, N_HEADS, dk, dv), F32),
            jax.ShapeDtypeStruct((n_seq, N_HEADS, 8, dk), F32),
            jax.ShapeDtypeStruct((n_seq, N_HEADS, 8, 128), F32),
        ],
        scratch_shapes=[pltpu.VMEM((N_HEADS, dk, dv), F32), pltpu.VMEM((N_HEADS, dk), F32),
                        pltpu.VMEM((N_HEADS, 1), F32)],
        input_output_aliases=aliases,
        compiler_params=_params(2),
        name="mlstm",
    )(*args)
    return yb, c_p, n_p[:, :, 0, :], m_p[:, :, 0, 0]


def _out_proj_kernel(ya_ref, yb_ref, sa_ref, sb_ref, x_ref, wa_ref, wb_ref, wo_ref, o_ref):
    d = o_ref.shape[1]
    ya = _dot(ya_ref[...], wa_ref[...])
    yb = _dot(yb_ref[...], wb_ref[...])
    mix = (sa_ref[...].astype(F32) * ya + sb_ref[...].astype(F32) * yb).astype(BF16)
    for c0, c1 in _chunks(d, COL_CHUNK):
        o_ref[:, c0:c1] = x_ref[:, c0:c1] + _dot(mix, wo_ref[:, c0:c1])


def _out_proj(ya_in, yb_in, z2, x, w_a, w_b, w_o, *, layer, gate_col0):
    t, d = x.shape
    da, db = ya_in.shape[1], yb_in.shape[1]
    ga_blk = gate_col0 // d
    tm = _row_tile(t, 2 * da + 2 * db + 2 * d * z2.dtype.itemsize + 4 * d + 4 * d, 2 * (da + db + d) * d)
    return pl.pallas_call(
        _out_proj_kernel,
        grid=(t // tm,),
        in_specs=[
            pl.BlockSpec((tm, da), lambda i: (i, 0)),
            pl.BlockSpec((tm, db), lambda i: (i, 0)),
            pl.BlockSpec((tm, d), lambda i: (i, ga_blk)),
            pl.BlockSpec((tm, d), lambda i: (i, ga_blk + 1)),
            pl.BlockSpec((tm, d), lambda i: (i, 0)),
            _resident((None, da, d), layer), _resident((None, db, d), layer), _resident((None, d, d), layer),
        ],
        out_specs=pl.BlockSpec((tm, d), lambda i: (i, 0)),
        out_shape=jax.ShapeDtypeStruct((t, d), F32),
        compiler_params=_params(1),
        name="out_proj",
    )(ya_in, yb_in, z2, z2, x, w_a, w_b, w_o)


def kernel(x_prompt, x_sample, state_conv, state_mlstm_c, state_mlstm_n, state_mlstm_m, g_ffn1, w_ffn1_gu, w_ffn1_down, g_mix, w_in, w_conv, b_igate, b_fgate, g_head, w_proj_a, w_proj_b, w_out, g_ffn2, w_ffn2_gu, w_ffn2_down, g_final):
    n_seq, seq_len, d = x_prompt.shape
    n_dec = x_sample.shape[0]
    depth = w_in.shape[0]
    d_conv = w_conv.shape[2]
    d_mlstm = g_head.shape[1]
    dv = d_mlstm // N_HEADS
    dk = state_mlstm_n.shape[3]
    d_qk = N_HEADS * dk
    assert x_sample.shape[1] == 1 and state_conv.shape[2] == CONV_W - 1

    c_q = 3 * d_conv
    c_if = c_q + 2 * d_qk + 2 * d_mlstm
    c_ga = c_if + 2 * N_HEADS
    plain_cols = 2 * d_qk + d_mlstm
    gate_col0 = plain_cols + d_mlstm
    q_scale = float(dk) ** -0.5
    gate_cols = slice(GATE_COL0, GATE_COL0 + 2 * N_HEADS)

    w1_gu, w1_d = w_ffn1_gu.astype(BF16), w_ffn1_down.astype(BF16)
    w2_gu, w2_d = w_ffn2_gu.astype(BF16), w_ffn2_down.astype(BF16)
    cast_blk = math.gcd(c_q, c_if - c_q)
    while cast_blk > 2048 and cast_blk % (2 * GATE_LANES) == 0:
        cast_blk //= 2
    w_in_t = jnp.swapaxes(w_in, 1, 2)
    w_conv3 = _cast_rows(w_in_t, 0, c_q, cast_blk)
    w_main = _cast_rows(w_in_t, c_q, c_if - c_q, cast_blk)
    assert c_ga - c_if == 2 * N_HEADS
    w_tail = jnp.pad(w_in_t[:, c_if:, :], ((0, 0), (GATE_COL0, 0), (0, 0))).astype(BF16)
    bias_if = jnp.pad(jnp.concatenate([b_igate, b_fgate], axis=1),
                      ((0, 0), (GATE_COL0, GATE_LANES - TAIL_SIG_ROW0)))[:, None, :]
    w_a, w_b, w_o = w_proj_a.astype(BF16), w_proj_b.astype(BF16), w_out.astype(BF16)
    g1, gmx, g2, gh = g_ffn1[:, None, :], g_mix[:, None, :], g_ffn2[:, None, :], g_head[:, None, :]
    g_fin = g_final.reshape(1, -1)

    xp = x_prompt.reshape(n_seq * seq_len, d)
    xs = x_sample.reshape(n_dec, d)

    state_elems = math.prod(state_mlstm_c.shape[1:])
    ffn_steps = xp.shape[0] // _ffn_tile(xp.shape[0], d, w_ffn1_down.shape[1], state_elems)
    n_slabs = 2 * ffn_steps

    conv_p, c_p, n_p, m_p = [], None, [], []
    conv_s, c_s, n_s, m_s = [], None, [], []
    for l in range(depth):
        last = l == depth - 1
        mix_kw = dict(layer=l, plain_cols=plain_cols, q_cols=d_qk, q_scale=q_scale)

        xs = _ffn(xs, g1, w1_gu, w1_d, g_fin, layer=l, final_norm=False)
        ya_s, u_new = _conv_proj_step(xs, gmx, w_conv3, w_conv, state_conv[l, :, 0, :], state_conv[l, :, 1, :],
                                      layer=l)
        z2_s, gates_s = _mix_proj(xs, gmx, w_main, w_tail, bias_if, out_dtype=F32, **mix_kw)
        pairs = _sample_pairs(z2_s, gates_s[:, gate_cols], state_mlstm_n[l], state_mlstm_m[l], g_head[l],
                              n_slabs=n_slabs, dk=dk, dv=dv)
        sample_state = lambda slab0: _SampleState(**pairs, c_state=state_mlstm_c, c_acc=c_s, layer=l, slab0=slab0)

        xp, yb_1, c_s, n_1, m_1 = _ffn(xp, g1, w1_gu, w1_d, g_fin, layer=l, final_norm=False,
                                       state=sample_state(0))
        ya_p, tail = _conv_proj_seq(xp, gmx, w_conv3, w_conv, layer=l, seq_len=seq_len)
        z2_p, gates_p = _mix_proj(xp, gmx, w_main, w_tail, bias_if, out_dtype=BF16, **mix_kw)
        gates_t = jnp.transpose(gates_p[:, gate_cols])
        yb_p, c_p, n_new_p, m_new_p = _mlstm(z2_p, gates_t, gh, c_p, layer=l, depth=depth, n_seq=n_seq,
                                             seq_len=seq_len, dk=dk, dv=dv)
        xp = _out_proj(ya_p, yb_p, z2_p, xp, w_a, w_b, w_o, layer=l, gate_col0=gate_col0)
        xp, yb_2, c_s, n_2, m_2 = _ffn(xp, g2, w2_gu, w2_d, g_fin, layer=l, final_norm=last,
                                       state=sample_state(ffn_steps))

        yb_s = jnp.concatenate([yb_1, yb_2]).reshape(n_dec, d_mlstm)
        xs = _out_proj(ya_s, yb_s, z2_s, xs, w_a, w_b, w_o, layer=l, gate_col0=gate_col0)
        xs = _ffn(xs, g2, w2_gu, w2_d, g_fin, layer=l, final_norm=last)

        conv_p.append(tail[:, 8 - (CONV_W - 1):, :])
        n_p.append(n_new_p)
        m_p.append(m_new_p)
        conv_s.append(jnp.stack([state_conv[l, :, 1, :], u_new], axis=1))
        n_s.append(jnp.concatenate([n_1, n_2]).reshape(n_dec, N_HEADS, dk))
        m_s.append(jnp.concatenate([m_1, m_2])[:, :, 0].reshape(n_dec, N_HEADS))

    return (xp.reshape(n_seq, seq_len, d), xs.reshape(n_dec, 1, d),
            jnp.stack(conv_p), c_p, jnp.stack(n_p), jnp.stack(m_p),
            jnp.stack(conv_s), c_s, jnp.stack(n_s), jnp.stack(m_s))
```

```python
import functools
import math
from typing import NamedTuple, Optional

import jax
import jax.numpy as jnp
from jax import lax
from jax.experimental import pallas as pl
from jax.experimental.pallas import tpu as pltpu

F32 = jnp.float32
BF16 = jnp.bfloat16

N_HEADS = 4
CONV_W = 3
EPS = 1e-6
MLSTM_CHUNK = 256
MLSTM_SEQ_GROUP = 1
GATE_LANES = 128
BF16_ROWS = 16
TAIL_SIG_ROW0 = BF16_ROWS * (-(-2 * N_HEADS // BF16_ROWS))
GATE_COL0 = TAIL_SIG_ROW0 - 2 * N_HEADS
TRANSPOSE_ROWS = 128
MXU_COLS = 256
COL_CHUNK = MXU_COLS
ROW_TILES = (1024, 512, 256, 128)
VMEM_LIMIT_BYTES = 56 * 1024 * 1024
VMEM_WINDOW_BUDGET = 48 * 1024 * 1024


def _row_tile(t, row_bytes, resident_bytes, multiple_of=None):
    for tm in ROW_TILES:
        if t % tm == 0 and (multiple_of is None or multiple_of % tm == 0):
            if resident_bytes + 2 * tm * row_bytes <= VMEM_WINDOW_BUDGET:
                return tm
    raise ValueError(f"no row tile fits {t} rows")


def _params(n_axes):
    return pltpu.CompilerParams(dimension_semantics=("arbitrary",) * n_axes,
                                vmem_limit_bytes=VMEM_LIMIT_BYTES)


def _resident(shape, layer):
    zeros = (0,) * (len(shape) - 1)
    return pl.BlockSpec(shape, lambda *_: (layer,) + zeros, pipeline_mode=pl.Buffered(1))


def _chunks(n, step):
    return [(c, min(c + step, n)) for c in range(0, n, step)]


def _rmsnorm(x, g):
    return x * lax.rsqrt(jnp.mean(x * x, axis=-1, keepdims=True) + EPS) * g


def _dot(a, b):
    return jnp.dot(a, b, preferred_element_type=F32)


def _sigmoid(x):
    return 0.5 * jnp.tanh(0.5 * x) + 0.5


def _dot_t(a, b_t):
    return lax.dot_general(a, b_t, (((1,), (1,)), ((), ())), preferred_element_type=F32)


def _cast_kernel(x_ref, o_ref):
    o_ref[...] = x_ref[...].astype(o_ref.dtype)


def _cast_rows(w_t, row0, n_rows, blk):
    depth, _, d = w_t.shape
    assert row0 % blk == 0 and n_rows % blk == 0
    return pl.pallas_call(
        _cast_kernel,
        grid=(depth, n_rows // blk),
        in_specs=[pl.BlockSpec((None, blk, d), lambda l, j: (l, row0 // blk + j, 0))],
        out_specs=pl.BlockSpec((None, blk, d), lambda l, j: (l, j, 0)),
        out_shape=jax.ShapeDtypeStruct((depth, n_rows, d), BF16),
        compiler_params=_params(2),
        name="cast_rows",
    )(w_t)


def _alternate(primary, secondary):
    for _ in primary:
        next(secondary, None)
    for _ in secondary:
        pass


def _ffn_phases(x_ref, g_ref, wgu_ref, wd_ref, gf_ref, o_ref, act_scr, *, final_norm):
    d = x_ref.shape[1]
    d_ff = wd_ref.shape[0]
    h = _rmsnorm(x_ref[...], g_ref[...]).astype(BF16)
    yield
    for c0, c1 in _chunks(d_ff, COL_CHUNK):
        gate = _dot(h, wgu_ref[:, c0:c1])
        up = _dot(h, wgu_ref[:, d_ff + c0:d_ff + c1])
        act_scr[:, c0:c1] = (gate * _sigmoid(gate) * up).astype(BF16)
        yield
    act = act_scr[...]
    for c0, c1 in _chunks(d, COL_CHUNK):
        o_ref[:, c0:c1] = x_ref[:, c0:c1] + 0.5 * _dot(act, wd_ref[:, c0:c1])
        yield
    if final_norm:
        o_ref[...] = _rmsnorm(o_ref[...], gf_ref[...])


def _ffn_kernel(*refs, final_norm, n_state_in, n_alias):
    ffn_in, state_in = refs[:5], refs[5:5 + n_state_in]
    outs = refs[5 + n_state_in + n_alias:]
    ffn = _ffn_phases(*ffn_in, outs[0], outs[-1], final_norm=final_norm)
    _alternate(ffn, _mlstm_step_phases(*state_in, *outs[1:-1]) if n_state_in else iter(()))


class _SampleState(NamedTuple):
    q: jax.Array
    k: jax.Array
    v: jax.Array
    so: jax.Array
    scal: jax.Array
    g_head: jax.Array
    c_state: jax.Array
    c_acc: Optional[jax.Array]
    layer: int
    slab0: int


def _ffn_tile(t, d, d_ff, state_elems):
    return _row_tile(t, 4 * d + 4 * d + d_ff + 4 * state_elems // t, 2 * 3 * d * d_ff)


def _ffn(x, g, w_gu, w_down, g_final, *, layer, final_norm, state=None):
    t, d = x.shape
    d_ff = w_down.shape[1]
    tm = _ffn_tile(t, d, d_ff, 0 if state is None else math.prod(state.c_state.shape[1:]))
    n_steps = t // tm
    in_specs = [
        pl.BlockSpec((tm, d), lambda i: (i, 0)),
        _resident((None, 1, d), layer),
        _resident((None, d, 2 * d_ff), layer),
        _resident((None, d_ff, d), layer),
        pl.BlockSpec((1, d), lambda i: (0, 0)),
    ]
    args = [x, g, w_gu, w_down, g_final]
    out_specs = [pl.BlockSpec((tm, d), lambda i: (i, 0))]
    out_shape = [jax.ShapeDtypeStruct((t, d), F32)]
    aliases = {}
    n_state_in = 0
    if state is not None:
        _, bb, dk = state.q.shape
        dv = state.v.shape[2]
        n_heads = state.c_state.shape[2]
        slab = lambda w: pl.BlockSpec((None, bb, w), lambda i: (state.slab0 + i, 0, 0))
        c_spec = pl.BlockSpec((None, bb // n_heads, n_heads, dk, dv),
                              lambda i: (state.layer, state.slab0 + i, 0, 0, 0))
        in_specs += [slab(dk), slab(dk), slab(dv), slab(dv), slab(state.scal.shape[2]),
                     pl.BlockSpec((bb, dv), lambda i: (0, 0)), c_spec]
        args += [state.q, state.k, state.v, state.so, state.scal, state.g_head, state.c_state]
        n_state_in = 7
        if state.c_acc is not None:
            in_specs.append(pl.BlockSpec(memory_space=pl.ANY))
            aliases = {len(args): 2}
            args.append(state.c_acc)
        out_slab = lambda w: pl.BlockSpec((None, bb, w), lambda i: (i, 0, 0))
        out_specs += [out_slab(dv), c_spec, out_slab(dk), out_slab(GATE_LANES)]
        out_shape += [jax.ShapeDtypeStruct((n_steps, bb, dv), BF16),
                      jax.ShapeDtypeStruct(state.c_state.shape, F32),
                      jax.ShapeDtypeStruct((n_steps, bb, dk), F32),
                      jax.ShapeDtypeStruct((n_steps, bb, GATE_LANES), F32)]
    kern = functools.partial(_ffn_kernel, final_norm=final_norm, n_state_in=n_state_in, n_alias=len(aliases))
    out = pl.pallas_call(
        kern,
        grid=(n_steps,),
        in_specs=in_specs,
        out_specs=out_specs,
        out_shape=out_shape,
        scratch_shapes=[pltpu.VMEM((tm, d_ff), BF16)],
        input_output_aliases=aliases,
        compiler_params=_params(1),
        name="ffn",
    )(*args)
    return out[0] if state is None else out


def _conv_branch(h, w3_ref, wconv_ref, ya_ref, tail_ref, ubuf, *, tm):
    dc = ya_ref.shape[1]
    w = wconv_ref[...]

    def project(c0, c1):
        return tuple(_dot_t(h, w3_ref[s * dc + c0:s * dc + c1, :]) for s in range(3))

    chunks = _chunks(dc, MXU_COLS)
    z_next = project(*chunks[0])
    for n, (c0, c1) in enumerate(chunks):
        zb, zc, za = z_next
        if n + 1 < len(chunks):
            z_next = project(*chunks[n + 1])
        u = zc * za
        ubuf[8:tm + 8, c0:c1] = u
        yc = (ubuf[6:tm + 6, c0:c1] * w[0:1, c0:c1] + ubuf[7:tm + 7, c0:c1] * w[1:2, c0:c1]
              + u * w[2:3, c0:c1])
        ya_ref[:, c0:c1] = (zb * yc).astype(BF16)
        tail_ref[0, :, c0:c1] = u[tm - 8:tm, :]


def _conv_step_kernel(x_ref, g_ref, w3_ref, wconv_ref, b0_ref, b1_ref, ya_ref, u_ref):
    dc = ya_ref.shape[1]
    h = _rmsnorm(x_ref[...], g_ref[...]).astype(BF16)
    zb = _dot_t(h, w3_ref[0:dc, :])
    u = _dot_t(h, w3_ref[dc:2 * dc, :]) * _dot_t(h, w3_ref[2 * dc:3 * dc, :])
    w = wconv_ref[...]
    yc = b0_ref[...] * w[0:1, :] + b1_ref[...] * w[1:2, :] + u * w[2:3, :]
    ya_ref[...] = (zb * yc).astype(BF16)
    u_ref[...] = u


def _conv_proj_step(x, g, w_conv3, w_conv, buf0, buf1, *, layer):
    t, d = x.shape
    dc = w_conv.shape[2]
    row = pl.BlockSpec((t, dc), lambda i: (0, 0))
    return pl.pallas_call(
        _conv_step_kernel,
        grid=(1,),
        in_specs=[
            pl.BlockSpec((t, d), lambda i: (0, 0)),
            _resident((None, 1, d), layer),
            _resident((None, 3 * dc, d), layer),
            _resident((None, CONV_W, dc), layer),
            row, row,
        ],
        out_specs=[row, row],
        out_shape=[jax.ShapeDtypeStruct((t, dc), BF16), jax.ShapeDtypeStruct((t, dc), F32)],
        compiler_params=_params(1),
        name="conv_proj_step",
    )(x, g, w_conv3, w_conv, buf0, buf1)


def _log_sigmoid(x):
    return jnp.minimum(x, 0.0) - jnp.log(1.0 + jnp.exp(-jnp.abs(x)))


def _mix_branch(h, wm_ref, wt_ref, bias_ref, z_ref, gate_ref, *, plain_cols, q_cols, q_scale):
    n_main = wm_ref.shape[0]
    n_sig_tail = wt_ref.shape[0] - TAIL_SIG_ROW0

    gz = _dot_t(h, wt_ref[0:GATE_LANES, :]) + bias_ref[...]
    lane = lax.broadcasted_iota(jnp.int32, gz.shape, 1)
    is_forget = (lane >= GATE_COL0 + N_HEADS) & (lane < GATE_COL0 + 2 * N_HEADS)
    gate_ref[...] = jnp.where(is_forget, _log_sigmoid(gz), gz)

    for c0, c1 in _chunks(n_main + n_sig_tail, COL_CHUNK):
        t0 = TAIL_SIG_ROW0 - n_main
        z = _dot_t(h, wm_ref[c0:c1, :]) if c0 < n_main else _dot_t(h, wt_ref[t0 + c0:t0 + c1, :])
        if c0 < q_cols:
            z = z * q_scale
        elif c0 >= plain_cols:
            z = _sigmoid(z)
        z_ref[:, c0:c1] = z.astype(z_ref.dtype)


def _mix_proj_kernel(x_ref, g_ref, wm_ref, wt_ref, bias_ref, z_ref, gate_ref, **mix_kw):
    h = _rmsnorm(x_ref[...], g_ref[...]).astype(BF16)
    _mix_branch(h, wm_ref, wt_ref, bias_ref, z_ref, gate_ref, **mix_kw)


def _in_proj_seq_kernel(x_ref, g_ref, w3_ref, wconv_ref, wm_ref, wt_ref, bias_ref,
                        ya_ref, tail_ref, z_ref, gate_ref, ubuf, *, tm, tiles_per_seq, **mix_kw):
    i = pl.program_id(0)

    @pl.when(i % tiles_per_seq == 0)
    def _():
        ubuf[0:8, :] = jnp.zeros((8, ubuf.shape[1]), F32)

    @pl.when(i % tiles_per_seq != 0)
    def _():
        ubuf[0:8, :] = ubuf[tm:tm + 8, :]

    h = _rmsnorm(x_ref[...], g_ref[...]).astype(BF16)
    _conv_branch(h, w3_ref, wconv_ref, ya_ref, tail_ref, ubuf, tm=tm)
    _mix_branch(h, wm_ref, wt_ref, bias_ref, z_ref, gate_ref, **mix_kw)


def _in_proj_seq(x, g, w_conv3, w_conv, w_main, w_tail, bias_if, *, layer, seq_len, plain_cols, q_cols, q_scale):
    t, d = x.shape
    dc = w_conv.shape[2]
    n_main, n_tail = w_main.shape[1], w_tail.shape[1]
    n = n_main + n_tail - TAIL_SIG_ROW0
    assert q_cols % COL_CHUNK == 0 and plain_cols % COL_CHUNK == 0 and n_main % COL_CHUNK == 0
    tm = _row_tile(t, 4 * d + 2 * dc + 2 * n + 4 * GATE_LANES + 2 * dc, 2 * (3 * dc + n_main + n_tail) * d,
                   multiple_of=seq_len)
    tiles_per_seq = seq_len // tm
    kern = functools.partial(_in_proj_seq_kernel, tm=tm, tiles_per_seq=tiles_per_seq,
                             plain_cols=plain_cols, q_cols=q_cols, q_scale=q_scale)
    return pl.pallas_call(
        kern,
        grid=(t // tm,),
        in_specs=[
            pl.BlockSpec((tm, d), lambda i: (i, 0)),
            _resident((None, 1, d), layer),
            _resident((None, 3 * dc, d), layer),
            _resident((None, CONV_W, dc), layer),
            _resident((None, n_main, d), layer),
            _resident((None, n_tail, d), layer),
            _resident((None, 1, GATE_LANES), layer),
        ],
        out_specs=[
            pl.BlockSpec((tm, dc), lambda i: (i, 0)),
            pl.BlockSpec((1, 8, dc), lambda i: (i // tiles_per_seq, 0, 0)),
            pl.BlockSpec((tm, n), lambda i: (i, 0)),
            pl.BlockSpec((tm, GATE_LANES), lambda i: (i, 0)),
        ],
        out_shape=[jax.ShapeDtypeStruct((t, dc), BF16),
                   jax.ShapeDtypeStruct((t // seq_len, 8, dc), F32),
                   jax.ShapeDtypeStruct((t, n), BF16),
                   jax.ShapeDtypeStruct((t, GATE_LANES), F32)],
        scratch_shapes=[pltpu.VMEM((tm + 8, dc), F32)],
        compiler_params=_params(1),
        name="in_proj_seq",
    )(x, g, w_conv3, w_conv, w_main, w_tail, bias_if)


def _mix_proj(x, g, w_main, w_tail, bias_if, *, layer, plain_cols, q_cols, q_scale, out_dtype):
    t, d = x.shape
    n_main, n_tail = w_main.shape[1], w_tail.shape[1]
    n = n_main + n_tail - TAIL_SIG_ROW0
    assert q_cols % COL_CHUNK == 0 and plain_cols % COL_CHUNK == 0 and n_main % COL_CHUNK == 0
    tm = _row_tile(t, 4 * d + n * jnp.dtype(out_dtype).itemsize + 4 * GATE_LANES, 2 * (n_main + n_tail) * d)
    kern = functools.partial(_mix_proj_kernel, plain_cols=plain_cols, q_cols=q_cols, q_scale=q_scale)
    return pl.pallas_call(
        kern,
        grid=(t // tm,),
        in_specs=[
            pl.BlockSpec((tm, d), lambda i: (i, 0)),
            _resident((None, 1, d), layer),
            _resident((None, n_main, d), layer),
            _resident((None, n_tail, d), layer),
            _resident((None, 1, GATE_LANES), layer),
        ],
        out_specs=[
            pl.BlockSpec((tm, n), lambda i: (i, 0)),
            pl.BlockSpec((tm, GATE_LANES), lambda i: (i, 0)),
        ],
        out_shape=[jax.ShapeDtypeStruct((t, n), out_dtype),
                   jax.ShapeDtypeStruct((t, GATE_LANES), F32)],
        compiler_params=_params(1),
        name="mix_proj",
    )(x, g, w_main, w_tail, bias_if)


def _mlstm_kernel(*refs, lc, n_chunks, dk, dv, n_alias):
    prompt_in = refs[:6]
    yb_ref, c_out, n_out, m_out, c_scr, n_scr, m_scr = refs[6 + n_alias:]
    c = pl.program_id(1)

    @pl.when(c == 0)
    def _():
        c_scr[...] = jnp.zeros(c_scr.shape, F32)
        n_scr[...] = jnp.zeros(n_scr.shape, F32)
        m_scr[...] = jnp.zeros(m_scr.shape, F32)

    _mlstm_chunk_body(*prompt_in, yb_ref, c_scr, n_scr, m_scr, lc=lc, dk=dk, dv=dv)

    @pl.when(c == n_chunks - 1)
    def _():
        n_grp, nh = c_out.shape[0], c_out.shape[1]
        for s in range(n_grp):
            for hd in range(nh):
                u = s * nh + hd
                c_out[s, hd] = c_scr[u]
                n_out[s, hd] = jnp.broadcast_to(n_scr[u:u + 1, :], n_out.shape[2:])
                m_out[s, hd] = jnp.broadcast_to(m_scr[u:u + 1, :], m_out.shape[2:])


def _mlstm_chunk_body(q_ref, k_ref, v_ref, so_ref, gt_ref, gh_ref, yb_ref, c_scr, n_scr, m_scr, *, lc, dk, dv):
    n_grp, nh = gt_ref.shape[0], gt_ref.shape[1] // 2
    nu = n_grp * nh
    units = [(s, hd) for s in range(n_grp) for hd in range(nh)]
    gates = gt_ref[...].reshape(n_grp * 2 * nh, lc)
    m_prev = m_scr[...]

    r_idx = lax.broadcasted_iota(jnp.int32, (lc, lc), 0)
    c_idx = lax.broadcasted_iota(jnp.int32, (lc, lc), 1)
    upper = (r_idx <= c_idx).astype(BF16)
    hi = gates.astype(BF16)
    rem = gates - hi.astype(F32)
    mid = rem.astype(BF16)
    lo = (rem - mid.astype(F32)).astype(BF16)
    cums = _dot(hi, upper) + _dot(mid, upper) + _dot(lo, upper)
    b_rows = jnp.concatenate([cums[(2 * s + 1) * nh:(2 * s + 2) * nh, :] for s in range(n_grp)], axis=0)
    a_rows = jnp.concatenate([gates[2 * s * nh:(2 * s + 1) * nh, :] for s in range(n_grp)], axis=0) - b_rows

    fill = (-nu) % 8
    g_rows = jnp.concatenate([a_rows, a_rows[:fill, :]], axis=0) if fill else a_rows
    lane = lax.broadcasted_iota(jnp.int32, g_rows.shape, 1)
    shift = 1
    while shift < lc:
        g_rows = jnp.maximum(g_rows, jnp.where(lane >= shift, pltpu.roll(g_rows, shift, axis=1), -jnp.inf))
        shift *= 2
    g_rows = jnp.maximum(g_rows[:nu, :], m_prev)
    g_last = g_rows[:, lc - 1:lc]
    m_scr[...] = b_rows[:, lc - 1:lc] + g_last
    decay = jnp.exp(m_prev - g_last)
    w_rows = jnp.exp(a_rows - g_last)

    assert 3 * nu <= TRANSPOSE_ROWS
    pad = jnp.zeros((TRANSPOSE_ROWS - 3 * nu, lc), F32)
    cols = jnp.transpose(jnp.concatenate([g_rows, w_rows, b_rows, pad], axis=0))
    g_col = [cols[:, u:u + 1] for u in range(nu)]
    w_col = [cols[:, nu + u:nu + u + 1] for u in range(nu)]
    b_col = [cols[:, 2 * nu + u:2 * nu + u + 1] for u in range(nu)]

    causal = c_idx <= r_idx
    n_all = n_scr[...]
    n_bf = n_all.astype(BF16)
    q = [q_ref[s, :, hd * dk:(hd + 1) * dk] for s, hd in units]
    k = [k_ref[s, :, hd * dk:(hd + 1) * dk] for s, hd in units]
    v = [v_ref[s, :, hd * dv:(hd + 1) * dv] for s, hd in units]
    trans_b = (((1,), (1,)), ((), ()))
    trans_a = (((0,), (0,)), ((), ()))

    qk, qc, qn, c_old = [], [], [], []
    for u in range(nu):
        qk.append(lax.dot_general(q[u], k[u], trans_b, preferred_element_type=F32))
        c_old.append(c_scr[u])
        qc.append(_dot(q[u], c_old[u].astype(BF16)))
        qn.append(lax.dot_general(q[u], n_bf, trans_b, preferred_element_type=F32)[:, u:u + 1])

    s_mat, sc = [], []
    for u in range(nu):
        dmat = jnp.exp(jnp.where(causal, a_rows[u:u + 1, :] - g_col[u], -jnp.inf))
        s_mat.append(qk[u] * dmat)
        sc.append(jnp.exp(m_prev[u:u + 1, :] - g_col[u]))

    for u, (s, hd) in enumerate(units):
        num = sc[u] * qc[u] + _dot(s_mat[u].astype(BF16), v[u])
        den = sc[u] * qn[u] + jnp.sum(s_mat[u], axis=1, keepdims=True)
        hraw = num / jnp.maximum(jnp.abs(den), jnp.exp(-(b_col[u] + g_col[u])))
        hn = (hraw * lax.rsqrt(jnp.mean(hraw * hraw, axis=1, keepdims=True) + EPS)
              * gh_ref[:, hd * dv:(hd + 1) * dv])
        yb_ref[s, :, hd * dv:(hd + 1) * dv] = (so_ref[s, :, hd * dv:(hd + 1) * dv].astype(F32) * hn).astype(BF16)

    for u in range(nu):
        wk = w_col[u] * k[u].astype(F32)
        dec = decay[u:u + 1, :]
        c_scr[u] = dec * c_old[u] + lax.dot_general(wk.astype(BF16), v[u], trans_a, preferred_element_type=F32)
        n_scr[u:u + 1, :] = dec * n_all[u:u + 1, :] + jnp.sum(wk, axis=0, keepdims=True)


def _mlstm_step_phases(q_ref, k_ref, v_ref, so_ref, gate_ref, gh_ref, c_ref, yb_ref, c_out, n_out, m_out):
    n_seq_blk, n_heads = c_ref.shape[0], c_ref.shape[1]
    bb = n_seq_blk * n_heads
    q = q_ref[...]
    k = k_ref[...]
    v = v_ref[...]
    dk = q.shape[1]
    i_pre = gate_ref[:, 0:1]
    log_f = gate_ref[:, 1:2]
    m_old = gate_ref[:, 2:3]
    n_old = gate_ref[:, GATE_LANES:GATE_LANES + dk]

    inter = log_f + m_old
    m_new = jnp.maximum(inter, i_pre)
    s_w = jnp.exp(i_pre - m_new)
    sc = jnp.exp(inter - m_new)
    s = jnp.sum(q * k, axis=1, keepdims=True) * s_w
    den = sc * jnp.sum(q * n_old, axis=1, keepdims=True) + s
    denom = jnp.maximum(jnp.abs(den), jnp.exp(-m_new))

    wk = s_w * k
    pad = jnp.zeros((TRANSPOSE_ROWS - bb, q.shape[1]), F32)
    q_t = jnp.transpose(jnp.concatenate([q, pad], axis=0))
    wk_t = jnp.transpose(jnp.concatenate([wk, pad], axis=0))
    n_out[...] = sc * n_old + wk
    m_out[...] = jnp.broadcast_to(m_new, m_out.shape)
    sc_rows = jnp.broadcast_to(sc, v.shape)
    yield

    qc_rows = []
    for j in range(bb):
        c_old = c_ref[j // n_heads, j % n_heads]
        qc_rows.append(jnp.sum(q_t[:, j:j + 1] * c_old, axis=0, keepdims=True))
        c_out[j // n_heads, j % n_heads] = sc_rows[j:j + 1, :] * c_old + wk_t[:, j:j + 1] * v[j:j + 1, :]
        yield
    qc = jnp.concatenate(qc_rows, axis=0)

    hraw = (sc * qc + s * v) / denom
    hn = hraw * lax.rsqrt(jnp.mean(hraw * hraw, axis=1, keepdims=True) + EPS) * gh_ref[...]
    yb_ref[...] = (so_ref[...] * hn).astype(BF16)
    yield


def _sample_pairs(z2_s, gates_s, n_state, m_state, g_head, *, n_slabs, dk, dv):
    n_dec = z2_s.shape[0]
    d_qk, d_v = N_HEADS * dk, N_HEADS * dv
    bb = n_dec * N_HEADS // n_slabs
    assert bb * n_slabs == n_dec * N_HEADS and bb % (8 * N_HEADS // math.gcd(8, N_HEADS)) == 0
    pairs = lambda a, w: a.reshape(n_slabs, bb, w)
    scal = jnp.stack([gates_s[:, :N_HEADS], gates_s[:, N_HEADS:], m_state], axis=-1)
    scal = jnp.pad(scal, ((0, 0), (0, 0), (0, GATE_LANES - 3)))
    return dict(
        q=pairs(z2_s[:, :d_qk], dk), k=pairs(z2_s[:, d_qk:2 * d_qk], dk),
        v=pairs(z2_s[:, 2 * d_qk:2 * d_qk + d_v], dv), so=pairs(z2_s[:, 2 * d_qk + d_v:2 * d_qk + 2 * d_v], dv),
        scal=pairs(jnp.concatenate([scal, n_state], axis=-1), GATE_LANES + dk),
        g_head=jnp.tile(g_head.reshape(N_HEADS, dv), (bb // N_HEADS, 1)))


def _mlstm(z2, gates_t, g_head, c_acc, *, layer, depth, n_seq, seq_len, dk, dv):
    lc = MLSTM_CHUNK
    grp = MLSTM_SEQ_GROUP
    n_chunks = seq_len // lc
    t = n_seq * seq_len
    d_qk, d_v = N_HEADS * dk, N_HEADS * dv
    assert d_v == 2 * d_qk and n_seq % grp == 0
    n_grp = n_seq // grp
    z2 = z2.reshape(n_grp, grp, seq_len, z2.shape[1])
    gates_t = gates_t.reshape(n_grp, grp, 2 * N_HEADS, seq_len)
    tok = lambda w, blk: pl.BlockSpec((None, grp, lc, w), lambda g, c: (g, 0, c, blk))
    in_specs = [
        tok(d_qk, 0), tok(d_qk, 1), tok(d_v, 1), tok(d_v, 2),
        pl.BlockSpec((None, grp, 2 * N_HEADS, lc), lambda g, c: (g, 0, 0, c)),
        _resident((None, 1, d_v), layer),
    ]
    args = [z2, z2, z2, z2, gates_t, g_head]
    aliases = {}
    if c_acc is not None:
        in_specs.append(pl.BlockSpec(memory_space=pl.ANY))
        aliases = {len(args): 1}
        args.append(c_acc)
    kern = functools.partial(_mlstm_kernel, lc=lc, n_chunks=n_chunks, dk=dk, dv=dv, n_alias=len(aliases))
    yb, c_p, n_p, m_p = pl.pallas_call(
        kern,
        grid=(n_grp, n_chunks),
        in_specs=in_specs,
        out_specs=[
            tok(d_v, 0),
            pl.BlockSpec((None, grp, N_HEADS, dk, dv), lambda g, c: (layer, g, 0, 0, 0)),
            pl.BlockSpec((grp, N_HEADS, 8, dk), lambda g, c: (g, 0, 0, 0)),
            pl.BlockSpec((grp, N_HEADS, 8, 128), lambda g, c: (g, 0, 0, 0)),
        ],
        out_shape=[
            jax.ShapeDtypeStruct((n_grp, grp, seq_len, d_v), BF16),
            jax.ShapeDtypeStruct((depth, n_seq, N_HEADS, dk, dv), F32),
            jax.ShapeDtypeStruct((n_seq, N_HEADS, 8, dk), F32),
            jax.ShapeDtypeStruct((n_seq, N_HEADS, 8, 128), F32),
        ],
        scratch_shapes=[pltpu.VMEM((grp * N_HEADS, dk, dv), F32), pltpu.VMEM((grp * N_HEADS, dk), F32),
                        pltpu.VMEM((grp * N_HEADS, 1), F32)],
        input_output_aliases=aliases,
        compiler_params=_params(2),
        name="mlstm",
    )(*args)
    return yb.reshape(t, d_v), c_p, n_p[:, :, 0, :], m_p[:, :, 0, 0]


def _out_proj_kernel(ya_ref, yb_ref, sa_ref, sb_ref, x_ref, wa_ref, wb_ref, wo_ref, o_ref):
    d = o_ref.shape[1]
    ya = _dot(ya_ref[...], wa_ref[...])
    yb = _dot(yb_ref[...], wb_ref[...])
    mix = (sa_ref[...].astype(F32) * ya + sb_ref[...].astype(F32) * yb).astype(BF16)
    for c0, c1 in _chunks(d, COL_CHUNK):
        o_ref[:, c0:c1] = x_ref[:, c0:c1] + _dot(mix, wo_ref[:, c0:c1])


def _out_proj(ya_in, yb_in, z2, x, w_a, w_b, w_o, *, layer, gate_col0):
    t, d = x.shape
    da, db = ya_in.shape[1], yb_in.shape[1]
    ga_blk = gate_col0 // d
    tm = _row_tile(t, 2 * da + 2 * db + 2 * d * z2.dtype.itemsize + 4 * d + 4 * d, 2 * (da + db + d) * d)
    return pl.pallas_call(
        _out_proj_kernel,
        grid=(t // tm,),
        in_specs=[
            pl.BlockSpec((tm, da), lambda i: (i, 0)),
            pl.BlockSpec((tm, db), lambda i: (i, 0)),
            pl.BlockSpec((tm, d), lambda i: (i, ga_blk)),
            pl.BlockSpec((tm, d), lambda i: (i, ga_blk + 1)),
            pl.BlockSpec((tm, d), lambda i: (i, 0)),
            _resident((None, da, d), layer), _resident((None, db, d), layer), _resident((None, d, d), layer),
        ],
        out_specs=pl.BlockSpec((tm, d), lambda i: (i, 0)),
        out_shape=jax.ShapeDtypeStruct((t, d), F32),
        compiler_params=_params(1),
        name="out_proj",
    )(ya_in, yb_in, z2, z2, x, w_a, w_b, w_o)


def kernel(x_prompt, x_sample, state_conv, state_mlstm_c, state_mlstm_n, state_mlstm_m, g_ffn1, w_ffn1_gu, w_ffn1_down, g_mix, w_in, w_conv, b_igate, b_fgate, g_head, w_proj_a, w_proj_b, w_out, g_ffn2, w_ffn2_gu, w_ffn2_down, g_final):
    n_seq, seq_len, d = x_prompt.shape
    n_dec = x_sample.shape[0]
    depth = w_in.shape[0]
    d_conv = w_conv.shape[2]
    d_mlstm = g_head.shape[1]
    dv = d_mlstm // N_HEADS
    dk = state_mlstm_n.shape[3]
    d_qk = N_HEADS * dk
    assert x_sample.shape[1] == 1 and state_conv.shape[2] == CONV_W - 1

    c_q = 3 * d_conv
    c_if = c_q + 2 * d_qk + 2 * d_mlstm
    c_ga = c_if + 2 * N_HEADS
    plain_cols = 2 * d_qk + d_mlstm
    gate_col0 = plain_cols + d_mlstm
    q_scale = float(dk) ** -0.5
    gate_cols = slice(GATE_COL0, GATE_COL0 + 2 * N_HEADS)

    w1_gu, w1_d = w_ffn1_gu.astype(BF16), w_ffn1_down.astype(BF16)
    w2_gu, w2_d = w_ffn2_gu.astype(BF16), w_ffn2_down.astype(BF16)
    cast_blk = math.gcd(c_q, c_if - c_q)
    while cast_blk > 2048 and cast_blk % (2 * GATE_LANES) == 0:
        cast_blk //= 2
    w_in_t = jnp.swapaxes(w_in, 1, 2)
    w_conv3 = _cast_rows(w_in_t, 0, c_q, cast_blk)
    w_main = _cast_rows(w_in_t, c_q, c_if - c_q, cast_blk)
    assert c_ga - c_if == 2 * N_HEADS
    w_tail = jnp.pad(w_in_t[:, c_if:, :], ((0, 0), (GATE_COL0, 0), (0, 0))).astype(BF16)
    bias_if = jnp.pad(jnp.concatenate([b_igate, b_fgate], axis=1),
                      ((0, 0), (GATE_COL0, GATE_LANES - TAIL_SIG_ROW0)))[:, None, :]
    w_a, w_b, w_o = w_proj_a.astype(BF16), w_proj_b.astype(BF16), w_out.astype(BF16)
    g1, gmx, g2, gh = g_ffn1[:, None, :], g_mix[:, None, :], g_ffn2[:, None, :], g_head[:, None, :]
    g_fin = g_final.reshape(1, -1)

    xp = x_prompt.reshape(n_seq * seq_len, d)
    xs = x_sample.reshape(n_dec, d)

    state_elems = math.prod(state_mlstm_c.shape[1:])
    ffn_steps = xp.shape[0] // _ffn_tile(xp.shape[0], d, w_ffn1_down.shape[1], state_elems)
    n_slabs = 2 * ffn_steps

    conv_p, c_p, n_p, m_p = [], None, [], []
    conv_s, c_s, n_s, m_s = [], None, [], []
    for l in range(depth):
        last = l == depth - 1
        mix_kw = dict(layer=l, plain_cols=plain_cols, q_cols=d_qk, q_scale=q_scale)

        xs = _ffn(xs, g1, w1_gu, w1_d, g_fin, layer=l, final_norm=False)
        ya_s, u_new = _conv_proj_step(xs, gmx, w_conv3, w_conv, state_conv[l, :, 0, :], state_conv[l, :, 1, :],
                                      layer=l)
        z2_s, gates_s = _mix_proj(xs, gmx, w_main, w_tail, bias_if, out_dtype=F32, **mix_kw)
        pairs = _sample_pairs(z2_s, gates_s[:, gate_cols], state_mlstm_n[l], state_mlstm_m[l], g_head[l],
                              n_slabs=n_slabs, dk=dk, dv=dv)
        sample_state = lambda slab0: _SampleState(**pairs, c_state=state_mlstm_c, c_acc=c_s, layer=l, slab0=slab0)

        xp, yb_1, c_s, n_1, m_1 = _ffn(xp, g1, w1_gu, w1_d, g_fin, layer=l, final_norm=False,
                                       state=sample_state(0))
        ya_p, tail, z2_p, gates_p = _in_proj_seq(xp, gmx, w_conv3, w_conv, w_main, w_tail, bias_if,
                                                 seq_len=seq_len, **mix_kw)
        gates_t = jnp.swapaxes(gates_p[:, gate_cols].reshape(n_seq, seq_len, 2 * N_HEADS), 1, 2)
        yb_p, c_p, n_new_p, m_new_p = _mlstm(z2_p, gates_t, gh, c_p, layer=l, depth=depth, n_seq=n_seq,
                                             seq_len=seq_len, dk=dk, dv=dv)
        xp = _out_proj(ya_p, yb_p, z2_p, xp, w_a, w_b, w_o, layer=l, gate_col0=gate_col0)
        xp, yb_2, c_s, n_2, m_2 = _ffn(xp, g2, w2_gu, w2_d, g_fin, layer=l, final_norm=last,
                                       state=sample_state(ffn_steps))

        yb_s = jnp.concatenate([yb_1, yb_2]).reshape(n_dec, d_mlstm)
        xs = _out_proj(ya_s, yb_s, z2_s, xs, w_a, w_b, w_o, layer=l, gate_col0=gate_col0)
        xs = _ffn(xs, g2, w2_gu, w2_d, g_fin, layer=l, final_norm=last)

        conv_p.append(tail[:, 8 - (CONV_W - 1):, :])
        n_p.append(n_new_p)
        m_p.append(m_new_p)
        conv_s.append(jnp.stack([state_conv[l, :, 1, :], u_new], axis=1))
        n_s.append(jnp.concatenate([n_1, n_2]).reshape(n_dec, N_HEADS, dk))
        m_s.append(jnp.concatenate([m_1, m_2])[:, :, 0].reshape(n_dec, N_HEADS))

    return (xp.reshape(n_seq, seq_len, d), xs.reshape(n_dec, 1, d),
            jnp.stack(conv_p), c_p, jnp.stack(n_p), jnp.stack(m_p),
            jnp.stack(conv_s), c_s, jnp.stack(n_s), jnp.stack(m_s))
```

```python
import functools
import math
from typing import NamedTuple, Optional

import jax
import jax.numpy as jnp
from jax import lax
from jax.experimental import pallas as pl
from jax.experimental.pallas import tpu as pltpu

F32 = jnp.float32
BF16 = jnp.bfloat16

N_HEADS = 4
CONV_W = 3
EPS = 1e-6
MLSTM_CHUNK = 512
MLSTM_SEQ_GROUP = 1
GATE_LANES = 128
BF16_ROWS = 16
TAIL_SIG_ROW0 = BF16_ROWS * (-(-2 * N_HEADS // BF16_ROWS))
GATE_COL0 = TAIL_SIG_ROW0 - 2 * N_HEADS
TRANSPOSE_ROWS = 128
MXU_COLS = 256
COL_CHUNK = MXU_COLS
ROW_TILES = (1024, 512, 256, 128)
VMEM_LIMIT_BYTES = 56 * 1024 * 1024
VMEM_WINDOW_BUDGET = 48 * 1024 * 1024


def _row_tile(t, row_bytes, resident_bytes, multiple_of=None):
    for tm in ROW_TILES:
        if t % tm == 0 and (multiple_of is None or multiple_of % tm == 0):
            if resident_bytes + 2 * tm * row_bytes <= VMEM_WINDOW_BUDGET:
                return tm
    raise ValueError(f"no row tile fits {t} rows")


def _params(n_axes):
    return pltpu.CompilerParams(dimension_semantics=("arbitrary",) * n_axes,
                                vmem_limit_bytes=VMEM_LIMIT_BYTES)


def _resident(shape, layer):
    zeros = (0,) * (len(shape) - 1)
    return pl.BlockSpec(shape, lambda *_: (layer,) + zeros, pipeline_mode=pl.Buffered(1))


def _chunks(n, step):
    return [(c, min(c + step, n)) for c in range(0, n, step)]


def _rmsnorm(x, g):
    return x * lax.rsqrt(jnp.mean(x * x, axis=-1, keepdims=True) + EPS) * g


def _dot(a, b):
    return jnp.dot(a, b, preferred_element_type=F32)


def _sigmoid(x):
    return 0.5 * jnp.tanh(0.5 * x) + 0.5


def _dot_t(a, b_t):
    return lax.dot_general(a, b_t, (((1,), (1,)), ((), ())), preferred_element_type=F32)


def _cast_kernel(x_ref, o_ref):
    o_ref[...] = x_ref[...].astype(o_ref.dtype)


def _cast_rows(w_t, row0, n_rows, blk):
    depth, _, d = w_t.shape
    assert row0 % blk == 0 and n_rows % blk == 0
    return pl.pallas_call(
        _cast_kernel,
        grid=(depth, n_rows // blk),
        in_specs=[pl.BlockSpec((None, blk, d), lambda l, j: (l, row0 // blk + j, 0))],
        out_specs=pl.BlockSpec((None, blk, d), lambda l, j: (l, j, 0)),
        out_shape=jax.ShapeDtypeStruct((depth, n_rows, d), BF16),
        compiler_params=_params(2),
        name="cast_rows",
    )(w_t)


def _alternate(primary, secondary):
    for _ in primary:
        next(secondary, None)
    for _ in secondary:
        pass


def _ffn_phases(x_ref, g_ref, wgu_ref, wd_ref, gf_ref, o_ref, act_scr, *, final_norm):
    d = x_ref.shape[1]
    d_ff = wd_ref.shape[0]
    h = _rmsnorm(x_ref[...], g_ref[...]).astype(BF16)
    yield
    for c0, c1 in _chunks(d_ff, COL_CHUNK):
        gate = _dot(h, wgu_ref[:, c0:c1])
        up = _dot(h, wgu_ref[:, d_ff + c0:d_ff + c1])
        act_scr[:, c0:c1] = (gate * _sigmoid(gate) * up).astype(BF16)
        yield
    act = act_scr[...]
    for c0, c1 in _chunks(d, COL_CHUNK):
        o_ref[:, c0:c1] = x_ref[:, c0:c1] + 0.5 * _dot(act, wd_ref[:, c0:c1])
        yield
    if final_norm:
        o_ref[...] = _rmsnorm(o_ref[...], gf_ref[...])


def _ffn_kernel(*refs, final_norm, n_state_in, n_alias):
    ffn_in, state_in = refs[:5], refs[5:5 + n_state_in]
    outs = refs[5 + n_state_in + n_alias:]
    ffn = _ffn_phases(*ffn_in, outs[0], outs[-1], final_norm=final_norm)
    _alternate(ffn, _mlstm_step_phases(*state_in, *outs[1:-1]) if n_state_in else iter(()))


class _SampleState(NamedTuple):
    q: jax.Array
    k: jax.Array
    v: jax.Array
    so: jax.Array
    scal: jax.Array
    g_head: jax.Array
    c_state: jax.Array
    c_acc: Optional[jax.Array]
    layer: int
    slab0: int


def _ffn_tile(t, d, d_ff, state_elems):
    return _row_tile(t, 4 * d + 4 * d + d_ff + 4 * state_elems // t, 2 * 3 * d * d_ff)


def _ffn(x, g, w_gu, w_down, g_final, *, layer, final_norm, state=None):
    t, d = x.shape
    d_ff = w_down.shape[1]
    tm = _ffn_tile(t, d, d_ff, 0 if state is None else math.prod(state.c_state.shape[1:]))
    n_steps = t // tm
    in_specs = [
        pl.BlockSpec((tm, d), lambda i: (i, 0)),
        _resident((None, 1, d), layer),
        _resident((None, d, 2 * d_ff), layer),
        _resident((None, d_ff, d), layer),
        pl.BlockSpec((1, d), lambda i: (0, 0)),
    ]
    args = [x, g, w_gu, w_down, g_final]
    out_specs = [pl.BlockSpec((tm, d), lambda i: (i, 0))]
    out_shape = [jax.ShapeDtypeStruct((t, d), F32)]
    aliases = {}
    n_state_in = 0
    if state is not None:
        _, bb, dk = state.q.shape
        dv = state.v.shape[2]
        n_heads = state.c_state.shape[2]
        slab = lambda w: pl.BlockSpec((None, bb, w), lambda i: (state.slab0 + i, 0, 0))
        c_spec = pl.BlockSpec((None, bb // n_heads, n_heads, dk, dv),
                              lambda i: (state.layer, state.slab0 + i, 0, 0, 0))
        in_specs += [slab(dk), slab(dk), slab(dv), slab(dv), slab(state.scal.shape[2]),
                     pl.BlockSpec((bb, dv), lambda i: (0, 0)), c_spec]
        args += [state.q, state.k, state.v, state.so, state.scal, state.g_head, state.c_state]
        n_state_in = 7
        if state.c_acc is not None:
            in_specs.append(pl.BlockSpec(memory_space=pl.ANY))
            aliases = {len(args): 2}
            args.append(state.c_acc)
        out_slab = lambda w: pl.BlockSpec((None, bb, w), lambda i: (i, 0, 0))
        out_specs += [out_slab(dv), c_spec, out_slab(dk), out_slab(GATE_LANES)]
        out_shape += [jax.ShapeDtypeStruct((n_steps, bb, dv), BF16),
                      jax.ShapeDtypeStruct(state.c_state.shape, F32),
                      jax.ShapeDtypeStruct((n_steps, bb, dk), F32),
                      jax.ShapeDtypeStruct((n_steps, bb, GATE_LANES), F32)]
    kern = functools.partial(_ffn_kernel, final_norm=final_norm, n_state_in=n_state_in, n_alias=len(aliases))
    out = pl.pallas_call(
        kern,
        grid=(n_steps,),
        in_specs=in_specs,
        out_specs=out_specs,
        out_shape=out_shape,
        scratch_shapes=[pltpu.VMEM((tm, d_ff), BF16)],
        input_output_aliases=aliases,
        compiler_params=_params(1),
        name="ffn",
    )(*args)
    return out[0] if state is None else out


def _conv_branch(h, w3_ref, wconv_ref, ya_ref, tail_ref, ubuf, *, tm):
    dc = ya_ref.shape[1]
    w = wconv_ref[...]

    def project(c0, c1):
        return tuple(_dot_t(h, w3_ref[s * dc + c0:s * dc + c1, :]) for s in range(3))

    chunks = _chunks(dc, MXU_COLS)
    z_next = project(*chunks[0])
    for n, (c0, c1) in enumerate(chunks):
        zb, zc, za = z_next
        if n + 1 < len(chunks):
            z_next = project(*chunks[n + 1])
        u = zc * za
        ubuf[8:tm + 8, c0:c1] = u
        yc = (ubuf[6:tm + 6, c0:c1] * w[0:1, c0:c1] + ubuf[7:tm + 7, c0:c1] * w[1:2, c0:c1]
              + u * w[2:3, c0:c1])
        ya_ref[:, c0:c1] = (zb * yc).astype(BF16)
        tail_ref[0, :, c0:c1] = u[tm - 8:tm, :]


def _conv_step_kernel(x_ref, g_ref, w3_ref, wconv_ref, b0_ref, b1_ref, ya_ref, u_ref):
    dc = ya_ref.shape[1]
    h = _rmsnorm(x_ref[...], g_ref[...]).astype(BF16)
    zb = _dot_t(h, w3_ref[0:dc, :])
    u = _dot_t(h, w3_ref[dc:2 * dc, :]) * _dot_t(h, w3_ref[2 * dc:3 * dc, :])
    w = wconv_ref[...]
    yc = b0_ref[...] * w[0:1, :] + b1_ref[...] * w[1:2, :] + u * w[2:3, :]
    ya_ref[...] = (zb * yc).astype(BF16)
    u_ref[...] = u


def _conv_proj_step(x, g, w_conv3, w_conv, buf0, buf1, *, layer):
    t, d = x.shape
    dc = w_conv.shape[2]
    row = pl.BlockSpec((t, dc), lambda i: (0, 0))
    return pl.pallas_call(
        _conv_step_kernel,
        grid=(1,),
        in_specs=[
            pl.BlockSpec((t, d), lambda i: (0, 0)),
            _resident((None, 1, d), layer),
            _resident((None, 3 * dc, d), layer),
            _resident((None, CONV_W, dc), layer),
            row, row,
        ],
        out_specs=[row, row],
        out_shape=[jax.ShapeDtypeStruct((t, dc), BF16), jax.ShapeDtypeStruct((t, dc), F32)],
        compiler_params=_params(1),
        name="conv_proj_step",
    )(x, g, w_conv3, w_conv, buf0, buf1)


def _log_sigmoid(x):
    return jnp.minimum(x, 0.0) - jnp.log(1.0 + jnp.exp(-jnp.abs(x)))


def _mix_branch(h, wm_ref, wt_ref, bias_ref, z_ref, gate_ref, *, plain_cols, q_cols, q_scale):
    n_main = wm_ref.shape[0]
    n_sig_tail = wt_ref.shape[0] - TAIL_SIG_ROW0

    gz = _dot_t(h, wt_ref[0:GATE_LANES, :]) + bias_ref[...]
    lane = lax.broadcasted_iota(jnp.int32, gz.shape, 1)
    is_forget = (lane >= GATE_COL0 + N_HEADS) & (lane < GATE_COL0 + 2 * N_HEADS)
    gate_ref[...] = jnp.where(is_forget, _log_sigmoid(gz), gz)

    for c0, c1 in _chunks(n_main + n_sig_tail, COL_CHUNK):
        t0 = TAIL_SIG_ROW0 - n_main
        z = _dot_t(h, wm_ref[c0:c1, :]) if c0 < n_main else _dot_t(h, wt_ref[t0 + c0:t0 + c1, :])
        if c0 < q_cols:
            z = z * q_scale
        elif c0 >= plain_cols:
            z = _sigmoid(z)
        z_ref[:, c0:c1] = z.astype(z_ref.dtype)


def _mix_proj_kernel(x_ref, g_ref, wm_ref, wt_ref, bias_ref, z_ref, gate_ref, **mix_kw):
    h = _rmsnorm(x_ref[...], g_ref[...]).astype(BF16)
    _mix_branch(h, wm_ref, wt_ref, bias_ref, z_ref, gate_ref, **mix_kw)


def _in_proj_seq_kernel(x_ref, g_ref, w3_ref, wconv_ref, wm_ref, wt_ref, bias_ref,
                        ya_ref, tail_ref, z_ref, gate_ref, ubuf, *, tm, tiles_per_seq, **mix_kw):
    i = pl.program_id(0)

    @pl.when(i % tiles_per_seq == 0)
    def _():
        ubuf[0:8, :] = jnp.zeros((8, ubuf.shape[1]), F32)

    @pl.when(i % tiles_per_seq != 0)
    def _():
        ubuf[0:8, :] = ubuf[tm:tm + 8, :]

    h = _rmsnorm(x_ref[...], g_ref[...]).astype(BF16)
    _conv_branch(h, w3_ref, wconv_ref, ya_ref, tail_ref, ubuf, tm=tm)
    _mix_branch(h, wm_ref, wt_ref, bias_ref, z_ref, gate_ref, **mix_kw)


def _in_proj_seq(x, g, w_conv3, w_conv, w_main, w_tail, bias_if, *, layer, seq_len, plain_cols, q_cols, q_scale):
    t, d = x.shape
    dc = w_conv.shape[2]
    n_main, n_tail = w_main.shape[1], w_tail.shape[1]
    n = n_main + n_tail - TAIL_SIG_ROW0
    assert q_cols % COL_CHUNK == 0 and plain_cols % COL_CHUNK == 0 and n_main % COL_CHUNK == 0
    tm = _row_tile(t, 4 * d + 2 * dc + 2 * n + 4 * GATE_LANES + 2 * dc, 2 * (3 * dc + n_main + n_tail) * d,
                   multiple_of=seq_len)
    tiles_per_seq = seq_len // tm
    kern = functools.partial(_in_proj_seq_kernel, tm=tm, tiles_per_seq=tiles_per_seq,
                             plain_cols=plain_cols, q_cols=q_cols, q_scale=q_scale)
    return pl.pallas_call(
        kern,
        grid=(t // tm,),
        in_specs=[
            pl.BlockSpec((tm, d), lambda i: (i, 0)),
            _resident((None, 1, d), layer),
            _resident((None, 3 * dc, d), layer),
            _resident((None, CONV_W, dc), layer),
            _resident((None, n_main, d), layer),
            _resident((None, n_tail, d), layer),
            _resident((None, 1, GATE_LANES), layer),
        ],
        out_specs=[
            pl.BlockSpec((tm, dc), lambda i: (i, 0)),
            pl.BlockSpec((1, 8, dc), lambda i: (i // tiles_per_seq, 0, 0)),
            pl.BlockSpec((tm, n), lambda i: (i, 0)),
            pl.BlockSpec((tm, GATE_LANES), lambda i: (i, 0)),
        ],
        out_shape=[jax.ShapeDtypeStruct((t, dc), BF16),
                   jax.ShapeDtypeStruct((t // seq_len, 8, dc), F32),
                   jax.ShapeDtypeStruct((t, n), BF16),
                   jax.ShapeDtypeStruct((t, GATE_LANES), F32)],
        scratch_shapes=[pltpu.VMEM((tm + 8, dc), F32)],
        compiler_params=_params(1),
        name="in_proj_seq",
    )(x, g, w_conv3, w_conv, w_main, w_tail, bias_if)


def _mix_proj(x, g, w_main, w_tail, bias_if, *, layer, plain_cols, q_cols, q_scale, out_dtype):
    t, d = x.shape
    n_main, n_tail = w_main.shape[1], w_tail.shape[1]
    n = n_main + n_tail - TAIL_SIG_ROW0
    assert q_cols % COL_CHUNK == 0 and plain_cols % COL_CHUNK == 0 and n_main % COL_CHUNK == 0
    tm = _row_tile(t, 4 * d + n * jnp.dtype(out_dtype).itemsize + 4 * GATE_LANES, 2 * (n_main + n_tail) * d)
    kern = functools.partial(_mix_proj_kernel, plain_cols=plain_cols, q_cols=q_cols, q_scale=q_scale)
    return pl.pallas_call(
        kern,
        grid=(t // tm,),
        in_specs=[
            pl.BlockSpec((tm, d), lambda i: (i, 0)),
            _resident((None, 1, d), layer),
            _resident((None, n_main, d), layer),
            _resident((None, n_tail, d), layer),
            _resident((None, 1, GATE_LANES), layer),
        ],
        out_specs=[
            pl.BlockSpec((tm, n), lambda i: (i, 0)),
            pl.BlockSpec((tm, GATE_LANES), lambda i: (i, 0)),
        ],
        out_shape=[jax.ShapeDtypeStruct((t, n), out_dtype),
                   jax.ShapeDtypeStruct((t, GATE_LANES), F32)],
        compiler_params=_params(1),
        name="mix_proj",
    )(x, g, w_main, w_tail, bias_if)


def _mlstm_kernel(*refs, lc, n_chunks, dk, dv, n_alias):
    prompt_in = refs[:6]
    yb_ref, c_out, n_out, m_out, c_scr, n_scr, m_scr = refs[6 + n_alias:]
    c = pl.program_id(1)

    @pl.when(c == 0)
    def _():
        c_scr[...] = jnp.zeros(c_scr.shape, F32)
        n_scr[...] = jnp.zeros(n_scr.shape, F32)
        m_scr[...] = jnp.zeros(m_scr.shape, F32)

    _mlstm_chunk_body(*prompt_in, yb_ref, c_scr, n_scr, m_scr, lc=lc, dk=dk, dv=dv)

    @pl.when(c == n_chunks - 1)
    def _():
        n_grp, nh = c_out.shape[0], c_out.shape[1]
        for s in range(n_grp):
            for hd in range(nh):
                u = s * nh + hd
                c_out[s, hd] = c_scr[u]
                n_out[s, hd] = jnp.broadcast_to(n_scr[u:u + 1, :], n_out.shape[2:])
                m_out[s, hd] = jnp.broadcast_to(m_scr[u:u + 1, :], m_out.shape[2:])


def _mlstm_chunk_body(q_ref, k_ref, v_ref, so_ref, gt_ref, gh_ref, yb_ref, c_scr, n_scr, m_scr, *, lc, dk, dv):
    n_grp, nh = gt_ref.shape[0], N_HEADS
    nu = n_grp * nh
    units = [(s, hd) for s in range(n_grp) for hd in range(nh)]
    gates = jnp.concatenate([jnp.transpose(gt_ref[s])[GATE_COL0:GATE_COL0 + 2 * nh, :] for s in range(n_grp)], axis=0)
    m_prev = m_scr[...]

    r_idx = lax.broadcasted_iota(jnp.int32, (lc, lc), 0)
    c_idx = lax.broadcasted_iota(jnp.int32, (lc, lc), 1)
    upper = (r_idx <= c_idx).astype(BF16)
    hi = gates.astype(BF16)
    rem = gates - hi.astype(F32)
    mid = rem.astype(BF16)
    lo = (rem - mid.astype(F32)).astype(BF16)
    cums = _dot(hi, upper) + _dot(mid, upper) + _dot(lo, upper)
    b_rows = jnp.concatenate([cums[(2 * s + 1) * nh:(2 * s + 2) * nh, :] for s in range(n_grp)], axis=0)
    a_rows = jnp.concatenate([gates[2 * s * nh:(2 * s + 1) * nh, :] for s in range(n_grp)], axis=0) - b_rows

    fill = (-nu) % 8
    g_rows = jnp.concatenate([a_rows, a_rows[:fill, :]], axis=0) if fill else a_rows
    lane = lax.broadcasted_iota(jnp.int32, g_rows.shape, 1)
    shift = 1
    while shift < lc:
        g_rows = jnp.maximum(g_rows, jnp.where(lane >= shift, pltpu.roll(g_rows, shift, axis=1), -jnp.inf))
        shift *= 2
    g_rows = jnp.maximum(g_rows[:nu, :], m_prev)
    g_last = g_rows[:, lc - 1:lc]
    m_scr[...] = b_rows[:, lc - 1:lc] + g_last
    decay = jnp.exp(m_prev - g_last)
    w_rows = jnp.exp(a_rows - g_last)

    assert 3 * nu <= TRANSPOSE_ROWS
    pad = jnp.zeros((TRANSPOSE_ROWS - 3 * nu, lc), F32)
    cols = jnp.transpose(jnp.concatenate([g_rows, w_rows, b_rows, pad], axis=0))
    g_col = [cols[:, u:u + 1] for u in range(nu)]
    w_col = [cols[:, nu + u:nu + u + 1] for u in range(nu)]
    b_col = [cols[:, 2 * nu + u:2 * nu + u + 1] for u in range(nu)]

    causal = c_idx <= r_idx
    n_all = n_scr[...]
    n_bf = n_all.astype(BF16)
    q = [q_ref[s, :, hd * dk:(hd + 1) * dk] for s, hd in units]
    k = [k_ref[s, :, hd * dk:(hd + 1) * dk] for s, hd in units]
    v = [v_ref[s, :, hd * dv:(hd + 1) * dv] for s, hd in units]
    trans_b = (((1,), (1,)), ((), ()))
    trans_a = (((0,), (0,)), ((), ()))

    qk, qc, qn, c_old, dmat, sc = [], [], [], [], [], []
    for u in range(nu):
        qk.append(lax.dot_general(q[u], k[u], trans_b, preferred_element_type=F32))
        c_old.append(c_scr[u])
        qc.append(_dot(q[u], c_old[u].astype(BF16)))
        qn.append(lax.dot_general(q[u], n_bf, trans_b, preferred_element_type=F32)[:, u:u + 1])
        dmat.append(jnp.exp(jnp.where(causal, a_rows[u:u + 1, :] - g_col[u], -jnp.inf)))
        sc.append(jnp.exp(m_prev[u:u + 1, :] - g_col[u]))

    s_mat = [qk[u] * dmat[u] for u in range(nu)]

    for u, (s, hd) in enumerate(units):
        num = sc[u] * qc[u] + _dot(s_mat[u].astype(BF16), v[u])
        den = sc[u] * qn[u] + jnp.sum(s_mat[u], axis=1, keepdims=True)
        hraw = num / jnp.maximum(jnp.abs(den), jnp.exp(-(b_col[u] + g_col[u])))
        hn = (hraw * lax.rsqrt(jnp.mean(hraw * hraw, axis=1, keepdims=True) + EPS)
              * gh_ref[:, hd * dv:(hd + 1) * dv])
        yb_ref[s, :, hd * dv:(hd + 1) * dv] = (so_ref[s, :, hd * dv:(hd + 1) * dv].astype(F32) * hn).astype(BF16)

    for u in range(nu):
        wk = w_col[u] * k[u].astype(F32)
        dec = decay[u:u + 1, :]
        c_scr[u] = dec * c_old[u] + lax.dot_general(wk.astype(BF16), v[u], trans_a, preferred_element_type=F32)
        n_scr[u:u + 1, :] = dec * n_all[u:u + 1, :] + jnp.sum(wk, axis=0, keepdims=True)


def _mlstm_step_phases(q_ref, k_ref, v_ref, so_ref, gate_ref, gh_ref, c_ref, yb_ref, c_out, n_out, m_out):
    n_seq_blk, n_heads = c_ref.shape[0], c_ref.shape[1]
    bb = n_seq_blk * n_heads
    q = q_ref[...]
    k = k_ref[...]
    v = v_ref[...]
    dk = q.shape[1]
    i_pre = gate_ref[:, 0:1]
    log_f = gate_ref[:, 1:2]
    m_old = gate_ref[:, 2:3]
    n_old = gate_ref[:, GATE_LANES:GATE_LANES + dk]

    inter = log_f + m_old
    m_new = jnp.maximum(inter, i_pre)
    s_w = jnp.exp(i_pre - m_new)
    sc = jnp.exp(inter - m_new)
    s = jnp.sum(q * k, axis=1, keepdims=True) * s_w
    den = sc * jnp.sum(q * n_old, axis=1, keepdims=True) + s
    denom = jnp.maximum(jnp.abs(den), jnp.exp(-m_new))

    wk = s_w * k
    pad = jnp.zeros((TRANSPOSE_ROWS - bb, q.shape[1]), F32)
    q_t = jnp.transpose(jnp.concatenate([q, pad], axis=0))
    wk_t = jnp.transpose(jnp.concatenate([wk, pad], axis=0))
    n_out[...] = sc * n_old + wk
    m_out[...] = jnp.broadcast_to(m_new, m_out.shape)
    sc_rows = jnp.broadcast_to(sc, v.shape)
    yield

    qc_rows = []
    for j in range(bb):
        c_old = c_ref[j // n_heads, j % n_heads]
        qc_rows.append(jnp.sum(q_t[:, j:j + 1] * c_old, axis=0, keepdims=True))
        c_out[j // n_heads, j % n_heads] = sc_rows[j:j + 1, :] * c_old + wk_t[:, j:j + 1] * v[j:j + 1, :]
        yield
    qc = jnp.concatenate(qc_rows, axis=0)

    hraw = (sc * qc + s * v) / denom
    hn = hraw * lax.rsqrt(jnp.mean(hraw * hraw, axis=1, keepdims=True) + EPS) * gh_ref[...]
    yb_ref[...] = (so_ref[...] * hn).astype(BF16)
    yield


def _sample_pairs(z2_s, gates_s, n_state, m_state, g_head, *, n_slabs, dk, dv):
    n_dec = z2_s.shape[0]
    d_qk, d_v = N_HEADS * dk, N_HEADS * dv
    bb = n_dec * N_HEADS // n_slabs
    assert bb * n_slabs == n_dec * N_HEADS and bb % (8 * N_HEADS // math.gcd(8, N_HEADS)) == 0
    pairs = lambda a, w: a.reshape(n_slabs, bb, w)
    scal = jnp.stack([gates_s[:, :N_HEADS], gates_s[:, N_HEADS:], m_state], axis=-1)
    scal = jnp.pad(scal, ((0, 0), (0, 0), (0, GATE_LANES - 3)))
    return dict(
        q=pairs(z2_s[:, :d_qk], dk), k=pairs(z2_s[:, d_qk:2 * d_qk], dk),
        v=pairs(z2_s[:, 2 * d_qk:2 * d_qk + d_v], dv), so=pairs(z2_s[:, 2 * d_qk + d_v:2 * d_qk + 2 * d_v], dv),
        scal=pairs(jnp.concatenate([scal, n_state], axis=-1), GATE_LANES + dk),
        g_head=jnp.tile(g_head.reshape(N_HEADS, dv), (bb // N_HEADS, 1)))


def _mlstm(z2, gates, g_head, c_acc, *, layer, depth, n_seq, seq_len, dk, dv):
    lc = MLSTM_CHUNK
    grp = MLSTM_SEQ_GROUP
    n_chunks = seq_len // lc
    t = n_seq * seq_len
    d_qk, d_v = N_HEADS * dk, N_HEADS * dv
    assert d_v == 2 * d_qk and n_seq % grp == 0
    n_grp = n_seq // grp
    z2 = z2.reshape(n_grp, grp, seq_len, z2.shape[1])
    gates = gates.reshape(n_grp, grp, seq_len, GATE_LANES)
    tok = lambda w, blk: pl.BlockSpec((None, grp, lc, w), lambda g, c: (g, 0, c, blk))
    in_specs = [
        tok(d_qk, 0), tok(d_qk, 1), tok(d_v, 1), tok(d_v, 2), tok(GATE_LANES, 0),
        _resident((None, 1, d_v), layer),
    ]
    args = [z2, z2, z2, z2, gates, g_head]
    aliases = {}
    if c_acc is not None:
        in_specs.append(pl.BlockSpec(memory_space=pl.ANY))
        aliases = {len(args): 1}
        args.append(c_acc)
    kern = functools.partial(_mlstm_kernel, lc=lc, n_chunks=n_chunks, dk=dk, dv=dv, n_alias=len(aliases))
    yb, c_p, n_p, m_p = pl.pallas_call(
        kern,
        grid=(n_grp, n_chunks),
        in_specs=in_specs,
        out_specs=[
            tok(d_v, 0),
            pl.BlockSpec((None, grp, N_HEADS, dk, dv), lambda g, c: (layer, g, 0, 0, 0)),
            pl.BlockSpec((grp, N_HEADS, 8, dk), lambda g, c: (g, 0, 0, 0)),
            pl.BlockSpec((grp, N_HEADS, 8, 128), lambda g, c: (g, 0, 0, 0)),
        ],
        out_shape=[
            jax.ShapeDtypeStruct((n_grp, grp, seq_len, d_v), BF16),
            jax.ShapeDtypeStruct((depth, n_seq, N_HEADS, dk, dv), F32),
            jax.ShapeDtypeStruct((n_seq, N_HEADS, 8, dk), F32),
            jax.ShapeDtypeStruct((n_seq, N_HEADS, 8, 128), F32),
        ],
        scratch_shapes=[pltpu.VMEM((grp * N_HEADS, dk, dv), F32), pltpu.VMEM((grp * N_HEADS, dk), F32),
                        pltpu.VMEM((grp * N_HEADS, 1), F32)],
        input_output_aliases=aliases,
        compiler_params=_params(2),
        name="mlstm",
    )(*args)
    return yb.reshape(t, d_v), c_p, n_p[:, :, 0, :], m_p[:, :, 0, 0]


def _out_proj_kernel(ya_ref, yb_ref, sa_ref, sb_ref, x_ref, wa_ref, wb_ref, wo_ref, o_ref):
    d = o_ref.shape[1]
    ya = _dot(ya_ref[...], wa_ref[...])
    yb = _dot(yb_ref[...], wb_ref[...])
    mix = (sa_ref[...].astype(F32) * ya + sb_ref[...].astype(F32) * yb).astype(BF16)
    for c0, c1 in _chunks(d, COL_CHUNK):
        o_ref[:, c0:c1] = x_ref[:, c0:c1] + _dot(mix, wo_ref[:, c0:c1])


def _out_proj(ya_in, yb_in, z2, x, w_a, w_b, w_o, *, layer, gate_col0):
    t, d = x.shape
    da, db = ya_in.shape[1], yb_in.shape[1]
    ga_blk = gate_col0 // d
    tm = _row_tile(t, 2 * da + 2 * db + 2 * d * z2.dtype.itemsize + 4 * d + 4 * d, 2 * (da + db + d) * d)
    return pl.pallas_call(
        _out_proj_kernel,
        grid=(t // tm,),
        in_specs=[
            pl.BlockSpec((tm, da), lambda i: (i, 0)),
            pl.BlockSpec((tm, db), lambda i: (i, 0)),
            pl.BlockSpec((tm, d), lambda i: (i, ga_blk)),
            pl.BlockSpec((tm, d), lambda i: (i, ga_blk + 1)),
            pl.BlockSpec((tm, d), lambda i: (i, 0)),
            _resident((None, da, d), layer), _resident((None, db, d), layer), _resident((None, d, d), layer),
        ],
        out_specs=pl.BlockSpec((tm, d), lambda i: (i, 0)),
        out_shape=jax.ShapeDtypeStruct((t, d), F32),
        compiler_params=_params(1),
        name="out_proj",
    )(ya_in, yb_in, z2, z2, x, w_a, w_b, w_o)


def kernel(x_prompt, x_sample, state_conv, state_mlstm_c, state_mlstm_n, state_mlstm_m, g_ffn1, w_ffn1_gu, w_ffn1_down, g_mix, w_in, w_conv, b_igate, b_fgate, g_head, w_proj_a, w_proj_b, w_out, g_ffn2, w_ffn2_gu, w_ffn2_down, g_final):
    n_seq, seq_len, d = x_prompt.shape
    n_dec = x_sample.shape[0]
    depth = w_in.shape[0]
    d_conv = w_conv.shape[2]
    d_mlstm = g_head.shape[1]
    dv = d_mlstm // N_HEADS
    dk = state_mlstm_n.shape[3]
    d_qk = N_HEADS * dk
    assert x_sample.shape[1] == 1 and state_conv.shape[2] == CONV_W - 1

    c_q = 3 * d_conv
    c_if = c_q + 2 * d_qk + 2 * d_mlstm
    c_ga = c_if + 2 * N_HEADS
    plain_cols = 2 * d_qk + d_mlstm
    gate_col0 = plain_cols + d_mlstm
    q_scale = float(dk) ** -0.5
    gate_cols = slice(GATE_COL0, GATE_COL0 + 2 * N_HEADS)

    w1_gu, w1_d = w_ffn1_gu.astype(BF16), w_ffn1_down.astype(BF16)
    w2_gu, w2_d = w_ffn2_gu.astype(BF16), w_ffn2_down.astype(BF16)
    cast_blk = math.gcd(c_q, c_if - c_q)
    while cast_blk > 2048 and cast_blk % (2 * GATE_LANES) == 0:
        cast_blk //= 2
    w_in_t = jnp.swapaxes(w_in, 1, 2)
    w_conv3 = _cast_rows(w_in_t, 0, c_q, cast_blk)
    w_main = _cast_rows(w_in_t, c_q, c_if - c_q, cast_blk)
    assert c_ga - c_if == 2 * N_HEADS
    w_tail = jnp.pad(w_in_t[:, c_if:, :], ((0, 0), (GATE_COL0, 0), (0, 0))).astype(BF16)
    bias_if = jnp.pad(jnp.concatenate([b_igate, b_fgate], axis=1),
                      ((0, 0), (GATE_COL0, GATE_LANES - TAIL_SIG_ROW0)))[:, None, :]
    w_a, w_b, w_o = w_proj_a.astype(BF16), w_proj_b.astype(BF16), w_out.astype(BF16)
    g1, gmx, g2, gh = g_ffn1[:, None, :], g_mix[:, None, :], g_ffn2[:, None, :], g_head[:, None, :]
    g_fin = g_final.reshape(1, -1)

    xp = x_prompt.reshape(n_seq * seq_len, d)
    xs = x_sample.reshape(n_dec, d)

    state_elems = math.prod(state_mlstm_c.shape[1:])
    ffn_steps = xp.shape[0] // _ffn_tile(xp.shape[0], d, w_ffn1_down.shape[1], state_elems)
    n_slabs = 2 * ffn_steps

    conv_p, c_p, n_p, m_p = [], None, [], []
    conv_s, c_s, n_s, m_s = [], None, [], []
    for l in range(depth):
        last = l == depth - 1
        mix_kw = dict(layer=l, plain_cols=plain_cols, q_cols=d_qk, q_scale=q_scale)

        xs = _ffn(xs, g1, w1_gu, w1_d, g_fin, layer=l, final_norm=False)
        ya_s, u_new = _conv_proj_step(xs, gmx, w_conv3, w_conv, state_conv[l, :, 0, :], state_conv[l, :, 1, :],
                                      layer=l)
        z2_s, gates_s = _mix_proj(xs, gmx, w_main, w_tail, bias_if, out_dtype=F32, **mix_kw)
        pairs = _sample_pairs(z2_s, gates_s[:, gate_cols], state_mlstm_n[l], state_mlstm_m[l], g_head[l],
                              n_slabs=n_slabs, dk=dk, dv=dv)
        sample_state = lambda slab0: _SampleState(**pairs, c_state=state_mlstm_c, c_acc=c_s, layer=l, slab0=slab0)

        xp, yb_1, c_s, n_1, m_1 = _ffn(xp, g1, w1_gu, w1_d, g_fin, layer=l, final_norm=False,
                                       state=sample_state(0))
        ya_p, tail, z2_p, gates_p = _in_proj_seq(xp, gmx, w_conv3, w_conv, w_main, w_tail, bias_if,
                                                 seq_len=seq_len, **mix_kw)
        yb_p, c_p, n_new_p, m_new_p = _mlstm(z2_p, gates_p, gh, c_p, layer=l, depth=depth, n_seq=n_seq,
                                             seq_len=seq_len, dk=dk, dv=dv)
        xp = _out_proj(ya_p, yb_p, z2_p, xp, w_a, w_b, w_o, layer=l, gate_col0=gate_col0)
        xp, yb_2, c_s, n_2, m_2 = _ffn(xp, g2, w2_gu, w2_d, g_fin, layer=l, final_norm=last,
                                       state=sample_state(ffn_steps))

        yb_s = jnp.concatenate([yb_1, yb_2]).reshape(n_dec, d_mlstm)
        xs = _out_proj(ya_s, yb_s, z2_s, xs, w_a, w_b, w_o, layer=l, gate_col0=gate_col0)
        xs = _ffn(xs, g2, w2_gu, w2_d, g_fin, layer=l, final_norm=last)

        conv_p.append(tail[:, 8 - (CONV_W - 1):, :])
        n_p.append(n_new_p)
        m_p.append(m_new_p)
        conv_s.append(jnp.stack([state_conv[l, :, 1, :], u_new], axis=1))
        n_s.append(jnp.concatenate([n_1, n_2]).reshape(n_dec, N_HEADS, dk))
        m_s.append(jnp.concatenate([m_1, m_2])[:, :, 0].reshape(n_dec, N_HEADS))

    return (xp.reshape(n_seq, seq_len, d), xs.reshape(n_dec, 1, d),
            jnp.stack(conv_p), c_p, jnp.stack(n_p), jnp.stack(m_p),
            jnp.stack(conv_s), c_s, jnp.stack(n_s), jnp.stack(m_s))
```
